```python
import jax, jax.numpy as jnp
from jax import lax
import numpy as np

D_MODEL = 1024
BATCH = 16
SEQ = 2048
DEPTH = 2

CTX_LEN = 256
GRID_W = 64
RET_DK = 128
RET_DV = 128
RET_HEADS = D_MODEL // RET_DK
RET_CHUNK = 128
NA_DH = 64
NA_HEADS = D_MODEL // NA_DH
NA_KH = 8
NA_KW = 16
LRU_W = D_MODEL
LRU_BLOCKS = 8
LRU_BS = LRU_W // LRU_BLOCKS
LRU_CONV = 4
LRU_C = 8.0
N_EXPERTS = 32
TOP_K = 4
D_EXPERT = D_MODEL
SWIGLU_LIMIT = 7.0
SWIGLU_ALPHA = 1.702
ROPE_BASE = 10000.0
NORM_EPS = 1e-6
N_BRANCH = 3
RET_W = RET_HEADS * RET_DK
RET_VW = RET_HEADS * RET_DV
NA_W = NA_HEADS * NA_DH
IN_SPLITS = (RET_W, RET_W, RET_VW, RET_VW, NA_W, NA_W, NA_W, LRU_W, LRU_W, D_MODEL, D_MODEL, D_MODEL)
IN_W = sum(IN_SPLITS)

kernel_name = "hybrid_retention_natten_rglru_moe_dit"


def _rmsnorm(t, g):
    tf = t.astype(jnp.float32)
    y = tf * lax.rsqrt(jnp.mean(tf * tf, axis=-1, keepdims=True) + NORM_EPS)
    return (y * g.astype(jnp.float32)).astype(t.dtype)


def _modulate(t, g, shift, scale):
    return _rmsnorm(t, g) * (1.0 + scale) + shift


def _split_cols(p):
    return jnp.split(p, np.cumsum(IN_SPLITS)[:-1].tolist(), axis=-1)


def _heads(t, n):
    b, tl, _ = t.shape
    return t.reshape(b, tl, n, -1).transpose(0, 2, 1, 3)


def _merge_heads(t):
    b, n, tl, d = t.shape
    return t.transpose(0, 2, 1, 3).reshape(b, tl, n * d)


def _rope_1d(u, pos):
    nf = u.shape[-1] // 2
    inv = ROPE_BASE ** (-jnp.arange(nf, dtype=jnp.float32) / nf)
    ang = pos.astype(jnp.float32)[:, None] * inv
    cos, sin = jnp.cos(ang).astype(u.dtype), jnp.sin(ang).astype(u.dtype)
    u1, u2 = u[..., :nf], u[..., nf:]
    return jnp.concatenate([u1 * cos - u2 * sin, u1 * sin + u2 * cos], axis=-1)


def _axial_rope(t, rows, cols):
    half = t.shape[-1] // 2
    return jnp.concatenate([_rope_1d(t[..., :half], rows), _rope_1d(t[..., half:], cols)], axis=-1)


def _retention_chunks(q, k, v, log_g, state0, include_diag):
    b, h, tl, dk = q.shape
    dv = v.shape[-1]
    n = tl // RET_CHUNK
    qc = q.reshape(b, h, n, RET_CHUNK, dk)
    kc = k.reshape(b, h, n, RET_CHUNK, dk)
    vc = v.reshape(b, h, n, RET_CHUNK, dv)
    pos = jnp.arange(RET_CHUNK, dtype=jnp.float32)
    diff = pos[:, None] - pos[None, :]
    mask = (diff >= 0) if include_diag else (diff > 0)
    decay = jnp.where(mask[None], jnp.exp(jnp.maximum(diff, 0.0)[None] * log_g[:, None, None]), 0.0)
    s = jnp.einsum('bhnid,bhnjd->bhnij', qc, kc) * decay[:, None]
    intra = jnp.einsum('bhnij,bhnje->bhnie', s, vc)
    zeta = jnp.exp((RET_CHUNK - 1.0 - pos)[None] * log_g[:, None])
    u = jnp.einsum('bhnjd,bhnje->nbhde', kc * zeta[:, None, :, None], vc)
    g_chunk = jnp.exp(RET_CHUNK * log_g)[:, None, None]

    def step(carry, u_n):
        return g_chunk * carry + u_n, carry

    _, s_prev = lax.scan(step, state0, u)
    xi = jnp.exp((pos + 1.0)[None] * log_g[:, None])
    cross = jnp.einsum('bhnid,nbhde->bhnie', qc * xi[:, None, :, None], s_prev)
    return (intra + cross).reshape(b, h, tl, dv)


def _retention_state(k, v, log_g):
    tl = k.shape[2]
    w = jnp.exp((tl - 1.0 - jnp.arange(tl, dtype=jnp.float32))[None] * log_g[:, None])
    return jnp.einsum('bhtd,bhte->bhde', k * w[None, :, :, None], v)


def _retention(qc, kc, vc, ql, kl, vl, logit_f, logit_b, ctx_out):
    lg_f = jax.nn.log_sigmoid(logit_f.astype(jnp.float32))
    lg_b = jax.nn.log_sigmoid(logit_b.astype(jnp.float32))
    fl = lambda t: jnp.flip(t, axis=2)
    s_f = _retention_state(kc, vc, lg_f)
    s_b = _retention_state(fl(kc), fl(vc), lg_b)
    yl = (_retention_chunks(ql, kl, vl, lg_f, s_f, True)
          + fl(_retention_chunks(fl(ql), fl(kl), fl(vl), lg_b, s_b, False)))
    yc = None
    if ctx_out:
        z = jnp.zeros_like(s_f)
        yc = (_retention_chunks(qc, kc, vc, lg_f, z, True)
              + fl(_retention_chunks(fl(qc), fl(kc), fl(vc), lg_b, z, False)))
    return yl, yc


def _head_rms(t):
    return t * lax.rsqrt(jnp.mean(t * t, axis=-1, keepdims=True) + NORM_EPS)


def _neighbourhood_attention(q, k, v, kc, vc, rpb, n_rows):
    b, h, tl, dh = q.shape
    kh, kw = min(NA_KH, n_rows), NA_KW
    grid = lambda t: t.reshape(b, h, n_rows, GRID_W, dh)
    qg, kg, vg = grid(q), grid(k), grid(v)
    cols = jnp.arange(GRID_W)
    col_idx = jnp.clip(cols - kw // 2, 0, GRID_W - kw)[:, None] + jnp.arange(kw)
    dc = col_idx - cols[:, None] + (NA_KW - 1)

    def row_block(r):
        r0 = jnp.clip(r - kh // 2, 0, n_rows - kh)
        kb = lax.dynamic_slice_in_dim(kg, r0, kh, axis=2)[:, :, :, col_idx]
        vb = lax.dynamic_slice_in_dim(vg, r0, kh, axis=2)[:, :, :, col_idx]
        qr = lax.dynamic_index_in_dim(qg, r, axis=2, keepdims=False)
        dr = r0 + jnp.arange(kh) - r + (NA_KH - 1)
        bias = rpb[:, dr[:, None, None], dc[None]].transpose(0, 2, 1, 3)
        s_loc = (jnp.einsum('bhwd,bhrwkd->bhwrk', qr, kb) + bias[None]).reshape(b, h, GRID_W, kh * kw)
        s_ctx = jnp.einsum('bhwd,bhmd->bhwm', qr, kc)
        p = jax.nn.softmax(jnp.concatenate([s_loc, s_ctx], axis=-1).astype(jnp.float32), axis=-1).astype(v.dtype)
        p_loc = p[..., :kh * kw].reshape(b, h, GRID_W, kh, kw)
        return (jnp.einsum('bhwrk,bhrwkd->bhwd', p_loc, vb)
                + jnp.einsum('bhwm,bhmd->bhwd', p[..., kh * kw:], vc))

    o = lax.map(row_block, jnp.arange(n_rows))
    return o.transpose(1, 2, 0, 3, 4).reshape(b, h, tl, dh)


def _ctx_attention(q, k, v):
    s = jnp.einsum('bhqd,bhkd->bhqk', q, k).astype(jnp.float32)
    p = jax.nn.softmax(s, axis=-1).astype(v.dtype)
    return jnp.einsum('bhqk,bhkd->bhqd', p, v)


def _centred_dwconv(t, w, bias):
    out = lax.conv_general_dilated(t, w[:, None, :], window_strides=(1,),
                                   padding=[((LRU_CONV - 1) // 2, LRU_CONV // 2)],
                                   dimension_numbers=('NWC', 'WIO', 'NWC'),
                                   feature_group_count=t.shape[-1])
    return out + bias


def _lin_combine(lhs, rhs):
    a1, b1 = lhs
    a2, b2 = rhs
    return a1 * a2, a2 * b1 + b2


def _rglru_scan(u, wa, ba, wx, bx, lam, h0):
    b, tl, _ = u.shape
    ub = u.reshape(b, tl, LRU_BLOCKS, LRU_BS)
    r = jax.nn.sigmoid(jnp.einsum('btkc,kcd->btkd', ub, wa).reshape(b, tl, LRU_W) + ba)
    i = jax.nn.sigmoid(jnp.einsum('btkc,kcd->btkd', ub, wx).reshape(b, tl, LRU_W) + bx)
    log_a = -LRU_C * r * jax.nn.softplus(-lam)
    a = jnp.exp(log_a)
    bterm = jnp.sqrt(-jnp.expm1(2.0 * log_a)) * (i * u)
    a_cum, h = lax.associative_scan(_lin_combine, (a, bterm), axis=1)
    return h + a_cum * h0[:, None, :]


def _rglru_mixer(xc, xl, conv_w, conv_b, wa, ba, wx, bx, lam, ctx_out):
    f32 = jnp.float32
    uc = _centred_dwconv(xc, conv_w, conv_b).astype(f32)
    ul = _centred_dwconv(xl, conv_w, conv_b).astype(f32)
    wa, ba, wx, bx, lam = (t.astype(f32) for t in (wa, ba, wx, bx, lam))
    fl = lambda t: jnp.flip(t, axis=1)
    z = jnp.zeros((xl.shape[0], LRU_W), f32)
    hc_f = _rglru_scan(uc, wa[0], ba[0], wx[0], bx[0], lam[0], z)
    hc_b = _rglru_scan(fl(uc), wa[1], ba[1], wx[1], bx[1], lam[1], z)
    hl = (_rglru_scan(ul, wa[0], ba[0], wx[0], bx[0], lam[0], hc_f[:, -1])
          + fl(_rglru_scan(fl(ul), wa[1], ba[1], wx[1], bx[1], lam[1], hc_b[:, -1])))
    hc = (hc_f + fl(hc_b)) if ctx_out else None
    return hl, hc


def _moe(t, w_router, b_router, w_up, b_up, w_down, b_down):
    logits = (t @ w_router + b_router).astype(jnp.float32)
    top_v, top_i = lax.top_k(logits, TOP_K)
    top_w = jax.nn.softmax(top_v, axis=-1)
    gates = jnp.einsum('nk,nke->ne', top_w, jax.nn.one_hot(top_i, N_EXPERTS, dtype=jnp.float32)).astype(t.dtype)
    out = jnp.zeros_like(t)
    for e in range(N_EXPERTS):
        hu = t @ w_up[e] + b_up[e]
        glu = jnp.minimum(hu[:, 0::2], SWIGLU_LIMIT)
        lin = jnp.clip(hu[:, 1::2], -SWIGLU_LIMIT, SWIGLU_LIMIT)
        act = glu * jax.nn.sigmoid(SWIGLU_ALPHA * glu) * (lin + 1.0)
        out = out + gates[:, e:e + 1] * (act @ w_down[e] + b_down[e])
    return out


def _layer(x, xc, c, c_ctx, w_mod, b_mod, norm1_g, norm2_g, w_mix_in, ret_decay_fwd, ret_decay_bwd,
           na_rel_bias, lru_conv_w, lru_conv_b, lru_gate_a_w, lru_gate_a_b, lru_gate_x_w, lru_gate_x_b,
           lru_lambda, w_branch, w_mix_out, w_router, b_router, w_expert_up, b_expert_up,
           w_expert_down, b_expert_down, last):
    f32 = jnp.float32
    b, tl, d = x.shape
    n_rows = tl // GRID_W
    tok = jnp.arange(tl)
    rows, cols = tok // GRID_W, tok % GRID_W
    need_ctx = not last

    mod = (jax.nn.silu(c) @ w_mod + b_mod)[:, None, :]
    mod_c = (jax.nn.silu(c_ctx) @ w_mod + b_mod)[None, None, :]
    sh1, sc1, g1, sh2, sc2, g2 = jnp.split(mod, 6, axis=-1)
    sh1c, sc1c, g1c, sh2c, sc2c, g2c = jnp.split(mod_c, 6, axis=-1)

    rq, rk, rv, rg, nq, nk, nv, lx, ly, ga, gb, gc = _split_cols(_modulate(x, norm1_g, sh1, sc1) @ w_mix_in)
    rqc, rkc, rvc, rgc, nqc, nkc, nvc, lxc, lyc, gac, gbc, gcc = _split_cols(
        _modulate(xc, norm1_g, sh1c, sc1c) @ w_mix_in)

    k_scale = RET_DK ** -0.5
    ret_l, ret_c = _retention(
        _heads(rqc, RET_HEADS).astype(f32), (_heads(rkc, RET_HEADS) * k_scale).astype(f32),
        _heads(rvc, RET_HEADS).astype(f32),
        _axial_rope(_heads(rq, RET_HEADS), rows, cols).astype(f32),
        (_axial_rope(_heads(rk, RET_HEADS), rows, cols) * k_scale).astype(f32),
        _heads(rv, RET_HEADS).astype(f32),
        ret_decay_fwd, ret_decay_bwd, need_ctx)

    q_scale = NA_DH ** -0.5
    kc_na, vc_na = _heads(nkc, NA_HEADS), _heads(nvc, NA_HEADS)
    na_l = _neighbourhood_attention(_heads(nq, NA_HEADS) * q_scale, _heads(nk, NA_HEADS), _heads(nv, NA_HEADS),
                                    kc_na, vc_na, na_rel_bias, n_rows)

    lru_l, lru_c = _rglru_mixer(lxc, lx, lru_conv_w, lru_conv_b, lru_gate_a_w, lru_gate_a_b,
                                lru_gate_x_w, lru_gate_x_b, lru_lambda, need_ctx)

    def merge(ret, rgate, na, lru, ygate, gate_a, gate_b, gate_c):
        out_a = (_merge_heads(_head_rms(ret)).astype(rgate.dtype) * jax.nn.silu(rgate)) @ w_branch[0]
        out_b = _merge_heads(na) @ w_branch[1]
        out_c = (lru.astype(ygate.dtype) * jax.nn.gelu(ygate)) @ w_branch[2]
        mixed = (jax.nn.sigmoid(gate_a) * out_a + jax.nn.sigmoid(gate_b) * out_b
                 + jax.nn.sigmoid(gate_c) * out_c)
        return mixed @ w_mix_out

    x = x + g1 * merge(ret_l, rg, na_l, lru_l, ly, ga, gb, gc)
    if need_ctx:
        na_c = _ctx_attention(_heads(nqc, NA_HEADS) * q_scale, kc_na, vc_na)
        xc = xc + g1c * merge(ret_c, rgc, _merge_heads(na_c).reshape(na_c.shape[0], NA_HEADS, -1, NA_DH) if False else na_c,
                              lru_c, lyc, gac, gbc, gcc)

    moe = lambda t: _moe(t, w_router, b_router, w_expert_up, b_expert_up, w_expert_down, b_expert_down)
    h2 = _modulate(x, norm2_g, sh2, sc2)
    if not need_ctx:
        return x + g2 * moe(h2.reshape(b * tl, d)).reshape(b, tl, d), xc
    h2c = _modulate(xc, norm2_g, sh2c, sc2c)
    n_ctx = xc.shape[1]
    y = moe(jnp.concatenate([h2c, h2], axis=1).reshape(-1, d)).reshape(b, n_ctx + tl, d)
    return x + g2 * y[:, n_ctx:], xc + g2c * y[:, :n_ctx]


def setup_inputs(seed: int = 0) -> dict:
    key = jax.random.key(seed)
    ks = jax.random.split(key, 32)
    f32 = jnp.float32
    nrm = lambda k, shape, scale: jax.random.normal(k, shape, f32) * scale
    d = D_MODEL
    ret_init = jnp.log(2.0 ** (5.0 + jnp.arange(RET_HEADS, dtype=f32)) - 1.0)
    u = jax.random.uniform(ks[18], (DEPTH, 2, LRU_W), f32, 0.9, 0.999)
    s = u ** (1.0 / LRU_C)
    return {
        "x": nrm(ks[0], (BATCH, SEQ, d), 1.0),
        "c": nrm(ks[1], (BATCH, d), 1.0),
        "ctx": nrm(ks[2], (BATCH, CTX_LEN, d), 1.0),
        "c_ctx": nrm(ks[3], (d,), 1.0),
        "w_mod": nrm(ks[4], (DEPTH, d, 6 * d), 0.5 * d ** -0.5),
        "b_mod": nrm(ks[5], (DEPTH, 6 * d), 0.02),
        "norm1_g": 1.0 + nrm(ks[6], (DEPTH, d), 0.01),
        "norm2_g": 1.0 + nrm(ks[7], (DEPTH, d), 0.01),
        "w_mix_in": nrm(ks[8], (DEPTH, d, IN_W), d ** -0.5),
        "ret_decay_fwd": ret_init + nrm(ks[9], (DEPTH, RET_HEADS), 0.01),
        "ret_decay_bwd": ret_init + nrm(ks[10], (DEPTH, RET_HEADS), 0.01),
        "na_rel_bias": nrm(ks[11], (DEPTH, NA_HEADS, 2 * NA_KH - 1, 2 * NA_KW - 1), 0.02),
        "lru_conv_w": nrm(ks[12], (DEPTH, LRU_CONV, LRU_W), LRU_CONV ** -0.5),
        "lru_conv_b": nrm(ks[13], (DEPTH, LRU_W), 0.01),
        "lru_gate_a_w": nrm(ks[14], (DEPTH, 2, LRU_BLOCKS, LRU_BS, LRU_BS), LRU_BS ** -0.5),
        "lru_gate_a_b": nrm(ks[15], (DEPTH, 2, LRU_W), 0.01),
        "lru_gate_x_w": nrm(ks[16], (DEPTH, 2, LRU_BLOCKS, LRU_BS, LRU_BS), LRU_BS ** -0.5),
        "lru_gate_x_b": nrm(ks[17], (DEPTH, 2, LRU_W), 0.01),
        "lru_lambda": jnp.log(s) - jnp.log1p(-s),
        "w_branch": nrm(ks[19], (DEPTH, N_BRANCH, d, d), d ** -0.5),
        "w_mix_out": nrm(ks[20], (DEPTH, d, d), d ** -0.5),
        "w_router": nrm(ks[21], (DEPTH, d, N_EXPERTS), d ** -0.5),
        "b_router": nrm(ks[22], (DEPTH, N_EXPERTS), 0.01),
        "w_expert_up": nrm(ks[23], (DEPTH, N_EXPERTS, d, 2 * D_EXPERT), d ** -0.5),
        "b_expert_up": nrm(ks[24], (DEPTH, N_EXPERTS, 2 * D_EXPERT), 0.01),
        "w_expert_down": nrm(ks[25], (DEPTH, N_EXPERTS, D_EXPERT, d), D_EXPERT ** -0.5),
        "b_expert_down": nrm(ks[26], (DEPTH, N_EXPERTS, d), 0.01),
        "final_norm_g": 1.0 + nrm(ks[27], (d,), 0.01),
    }


def reference(x, c, ctx, c_ctx, w_mod, b_mod, norm1_g, norm2_g, w_mix_in, ret_decay_fwd, ret_decay_bwd,
              na_rel_bias, lru_conv_w, lru_conv_b, lru_gate_a_w, lru_gate_a_b, lru_gate_x_w, lru_gate_x_b,
              lru_lambda, w_branch, w_mix_out, w_router, b_router, w_expert_up, b_expert_up,
              w_expert_down, b_expert_down, final_norm_g):
    xc = ctx
    for l in range(DEPTH):
        x, xc = _layer(x, xc, c, c_ctx, w_mod[l], b_mod[l], norm1_g[l], norm2_g[l], w_mix_in[l],
                       ret_decay_fwd[l], ret_decay_bwd[l], na_rel_bias[l], lru_conv_w[l], lru_conv_b[l],
                       lru_gate_a_w[l], lru_gate_a_b[l], lru_gate_x_w[l], lru_gate_x_b[l], lru_lambda[l],
                       w_branch[l], w_mix_out[l], w_router[l], b_router[l], w_expert_up[l], b_expert_up[l],
                       w_expert_down[l], b_expert_down[l], l == DEPTH - 1)
    return _rmsnorm(x, final_norm_g)
```

```python
import functools

import numpy as np
import jax
import jax.numpy as jnp
from jax import lax
from jax.experimental import pallas as pl
from jax.experimental.pallas import tpu as pltpu

F32 = jnp.float32
BF16 = jnp.bfloat16

D_MODEL = 1024
SEQ = 2048
CTX_LEN = 256
GRID_W = 64
N_ROWS = SEQ // GRID_W
LANES = 128
SUBLANES = 8
NCB = D_MODEL // LANES
N_SEC = 12
RET_HEADS = 8
RET_CHUNK = 256
NA_HEADS = 16
NA_KH = 8
NA_KW = 16
LRU_BLOCKS = 8
LRU_C = 8.0
N_EXPERTS = 32
TOP_K = 4
SWIGLU_LIMIT = 7.0
SWIGLU_ALPHA = 1.702
ROPE_BASE = 10000.0
NORM_EPS = 1e-6
NEG_BIG = -1e30
TOKEN_TILE = 1024
MERGE_TILE = 512
ROW_DMA_TILE = 256
EXPERT_TILE = 256
VMEM_LIMIT = 56 * 1024 * 1024

S_RQ, S_RK, S_RV, S_RG, S_NQ, S_NK, S_NV, S_LX, S_LY, S_GA, S_GB, S_GC = range(12)


def _nt_dot(a, b):
    return lax.dot_general(a, b, (((1,), (1,)), ((), ())), preferred_element_type=F32)


def _tn_dot(a, b):
    return lax.dot_general(a, b, (((0,), (0,)), ((), ())), preferred_element_type=F32)


def _params(sem, vmem=VMEM_LIMIT):
    return pltpu.CompilerParams(dimension_semantics=sem, vmem_limit_bytes=vmem)


def _mod_body(c_ref, w_ref, b_ref, o_ref):
    c = c_ref[...]
    s = c * jax.nn.sigmoid(c)
    o_ref[...] = jnp.dot(s, w_ref[...], preferred_element_type=F32,
                         precision=lax.Precision.HIGHEST) + b_ref[...]


def _modulation(cc, w_mod, b_mod):
    mp = cc.shape[0]
    out = pl.pallas_call(
        _mod_body,
        grid=(6,),
        in_specs=[pl.BlockSpec((mp, D_MODEL), lambda j: (0, 0)),
                  pl.BlockSpec((D_MODEL, D_MODEL), lambda j: (0, j)),
                  pl.BlockSpec((1, D_MODEL), lambda j: (0, j))],
        out_specs=pl.BlockSpec((mp, D_MODEL), lambda j: (0, j)),
        out_shape=jax.ShapeDtypeStruct((mp, 6 * D_MODEL), F32),
        compiler_params=_params(("arbitrary",)),
        name="adaln_mod",
    )(cc, w_mod, b_mod.reshape(1, 6 * D_MODEL))
    return out.reshape(mp, 6, D_MODEL)


def _inproj_body(x_ref, mod_ref, g_ref, w_ref, o_ref, h_ref):
    @pl.when(pl.program_id(1) == 0)
    def _():
        x = x_ref[...]
        ms = jnp.mean(x * x, axis=-1, keepdims=True)
        y = x * lax.rsqrt(ms + NORM_EPS) * g_ref[...]
        h_ref[...] = (y * (1.0 + mod_ref[0, 1:2, :]) + mod_ref[0, 0:1, :]).astype(BF16)

    acc = jnp.dot(h_ref[...], w_ref[...], preferred_element_type=F32)
    for g in range(NCB):
        o_ref[g] = acc[:, g * LANES:(g + 1) * LANES].astype(BF16)


def _input_projection(x, mod, mod_row, norm_g, w_in, tm):
    t = x.shape[0]
    return pl.pallas_call(
        _inproj_body,
        grid=(t // tm, N_SEC),
        in_specs=[pl.BlockSpec((tm, D_MODEL), lambda i, j: (i, 0)),
                  pl.BlockSpec((1, 6, D_MODEL), lambda i, j: (mod_row(i), 0, 0)),
                  pl.BlockSpec((1, D_MODEL), lambda i, j: (0, 0)),
                  pl.BlockSpec((D_MODEL, D_MODEL), lambda i, j: (0, j))],
        out_specs=pl.BlockSpec((NCB, tm, LANES), lambda i, j: (j, i, 0)),
        out_shape=jax.ShapeDtypeStruct((N_SEC * NCB, t, LANES), BF16),
        scratch_shapes=[pltpu.VMEM((tm, D_MODEL), BF16)],
        compiler_params=_params(("arbitrary", "arbitrary")),
        name="in_proj",
    )(x, mod, norm_g.reshape(1, D_MODEL), w_in)


def _rope(x, cos, sin_signed, first_half):
    partner = jnp.where(first_half, pltpu.roll(x, LANES - 32, 1), pltpu.roll(x, 32, 1))
    return x * cos + partner * sin_signed


def _head_rms_gate(o, gate):
    o = o * lax.rsqrt(jnp.mean(o * o, axis=-1, keepdims=True) + NORM_EPS)
    return o * (gate * jax.nn.sigmoid(gate))


def _ret_body(*refs, need_ctx):
    if need_ctx:
        (lg_ref, q_ref, k_ref, v_ref, rg_ref, kc_ref, vc_ref, cos_ref, sin_ref, qc_ref, rgc_ref,
         o_ref, oc_ref, kr_ref, sb_ref, dc_ref) = refs
    else:
        (lg_ref, q_ref, k_ref, v_ref, rg_ref, kc_ref, vc_ref, cos_ref, sin_ref,
         o_ref, kr_ref, sb_ref, dc_ref) = refs
    h = pl.program_id(1)
    c = RET_CHUNK
    n_chunks = SEQ // c
    lgf = lg_ref[0, h]
    lgb = lg_ref[1, h]
    ks = float(LANES) ** -0.5

    ia = lax.broadcasted_iota(jnp.int32, (c, 1), 0).astype(F32)
    ii = lax.broadcasted_iota(jnp.int32, (c, c), 0)
    jj = lax.broadcasted_iota(jnp.int32, (c, c), 1)
    diff = (ii - jj).astype(F32)
    dc_ref[...] = jnp.where(diff >= 0, jnp.exp(diff * lgf), jnp.exp(-diff * lgb))
    w_f = jnp.exp((c - 1.0 - ia) * lgf)
    w_b = jnp.exp(ia * lgb)
    x_f = jnp.exp((ia + 1.0) * lgf)
    x_b = jnp.exp((c - ia) * lgb)
    g_f = jnp.exp(c * lgf)
    g_b = jnp.exp(c * lgb)
    lane = lax.broadcasted_iota(jnp.int32, (1, LANES), 1)
    first_half = (lane % 64) < 32

    kc = kc_ref[0].astype(F32)
    vc = vc_ref[0]
    sf0 = _tn_dot((kc * w_f).astype(BF16), vc)
    sb0 = _tn_dot((kc * w_b).astype(BF16), vc)

    def bwd_body(i, sb):
        n = n_chunks - 1 - i
        r0 = pl.multiple_of(n * c, c)
        sb_ref[n] = sb
        k = _rope(k_ref[0, pl.ds(r0, c), :].astype(F32), cos_ref[pl.ds(r0, c), :], sin_ref[pl.ds(r0, c), :],
                  first_half)
        kr_ref[pl.ds(r0, c), :] = k.astype(BF16)
        ub = _tn_dot((k * w_b).astype(BF16), v_ref[0, pl.ds(r0, c), :])
        return ub + g_b * sb

    lax.fori_loop(0, n_chunks, bwd_body, sb0)

    def fwd_body(n, sf):
        r0 = pl.multiple_of(n * c, c)
        q = _rope(q_ref[0, pl.ds(r0, c), :].astype(F32), cos_ref[pl.ds(r0, c), :], sin_ref[pl.ds(r0, c), :],
                  first_half)
        k = kr_ref[pl.ds(r0, c), :]
        v = v_ref[0, pl.ds(r0, c), :]
        s = _nt_dot(q.astype(BF16), k)
        o = jnp.dot((s * dc_ref[...]).astype(BF16), v, preferred_element_type=F32)
        o = o + jnp.dot((q * x_f).astype(BF16), sf.astype(BF16), preferred_element_type=F32)
        o = o + jnp.dot((q * x_b).astype(BF16), sb_ref[n].astype(BF16), preferred_element_type=F32)
        o = _head_rms_gate(o * ks, rg_ref[0, pl.ds(r0, c), :].astype(F32))
        o_ref[0, pl.ds(r0, c), :] = o.astype(BF16)
        uf = _tn_dot((k.astype(F32) * w_f).astype(BF16), v)
        return uf + g_f * sf

    lax.fori_loop(0, n_chunks, fwd_body, sf0)

    if need_ctx:
        s = _nt_dot(qc_ref[0], kc_ref[0])
        o = jnp.dot((s * dc_ref[...]).astype(BF16), vc, preferred_element_type=F32)
        oc_ref[0] = _head_rms_gate(o * ks, rgc_ref[0].astype(F32)).astype(BF16)


def _retention(proj_l, proj_c, lg, cos, sin_signed, batch, need_ctx):
    def lat(sec):
        return pl.BlockSpec((1, SEQ, LANES), lambda b, h: (sec * NCB + h, b, 0))

    def ctx(sec):
        return pl.BlockSpec((1, CTX_LEN, LANES), lambda b, h: (sec * NCB + h, b, 0))

    table = pl.BlockSpec((SEQ, LANES), lambda b, h: (0, 0))
    in_specs = [pl.BlockSpec(memory_space=pltpu.SMEM),
                lat(S_RQ), lat(S_RK), lat(S_RV), lat(S_RG), ctx(S_RK), ctx(S_RV), table, table]
    args = [lg, proj_l, proj_l, proj_l, proj_l, proj_c, proj_c, cos, sin_signed]
    out_specs = [pl.BlockSpec((1, SEQ, LANES), lambda b, h: (h, b, 0))]
    out_shape = [jax.ShapeDtypeStruct((RET_HEADS, batch * SEQ, LANES), BF16)]
    if need_ctx:
        in_specs += [ctx(S_RQ), ctx(S_RG)]
        args += [proj_c, proj_c]
        out_specs.append(pl.BlockSpec((1, CTX_LEN, LANES), lambda b, h: (h, b, 0)))
        out_shape.append(jax.ShapeDtypeStruct((RET_HEADS, batch * CTX_LEN, LANES), BF16))
    return pl.pallas_call(
        functools.partial(_ret_body, need_ctx=need_ctx),
        grid=(batch, RET_HEADS),
        in_specs=in_specs,
        out_specs=out_specs,
        out_shape=out_shape,
        scratch_shapes=[pltpu.VMEM((SEQ, LANES), BF16),
                        pltpu.VMEM((SEQ // RET_CHUNK, LANES, LANES), F32),
                        pltpu.VMEM((RET_CHUNK, RET_CHUNK), F32)],
        compiler_params=_params(("arbitrary", "arbitrary")),
        name="retention",
    )(*args)


def _na_softmax_pv(s_parts, v_parts):
    m = functools.reduce(jnp.maximum, [jnp.max(s, axis=-1, keepdims=True) for s in s_parts])
    p_parts = [jnp.exp(s - m) for s in s_parts]
    den = functools.reduce(jnp.add, [jnp.sum(p, axis=-1, keepdims=True) for p in p_parts])
    o = functools.reduce(jnp.add, [jnp.dot(p.astype(BF16), v, preferred_element_type=F32)
                                   for p, v in zip(p_parts, v_parts)])
    return o / den


def _split_heads(q, low):
    zero = jnp.zeros_like(q)
    return jnp.concatenate([jnp.where(low, q, zero), jnp.where(low, zero, q)], axis=0)


def _na_body(*refs, need_ctx):
    if need_ctx:
        q_ref, k_ref, v_ref, kc_ref, vc_ref, bias_ref, qc_ref, o_ref, oc_ref = refs
    else:
        q_ref, k_ref, v_ref, kc_ref, vc_ref, bias_ref, o_ref = refs
    lane = lax.broadcasted_iota(jnp.int32, (1, LANES), 1)
    low = lane < 64
    scale = 64.0 ** -0.5
    kc = kc_ref[0]
    vc = vc_ref[0]
    win = NA_KH * GRID_W

    def row_body(r, carry):
        r0 = jnp.clip(r - NA_KH // 2, 0, N_ROWS - NA_KH)
        off = r0 - r + (NA_KH - 1)
        q0 = pl.multiple_of(r * GRID_W, GRID_W)
        k0 = pl.multiple_of(r0 * GRID_W, GRID_W)
        q2 = _split_heads(q_ref[0, pl.ds(q0, GRID_W), :], low)
        kw = k_ref[0, pl.ds(k0, win), :]
        vw = v_ref[0, pl.ds(k0, win), :]
        bias = jnp.concatenate([bias_ref[0, off], bias_ref[1, off]], axis=0)
        s_loc = _nt_dot(q2, kw) * scale + bias
        s_ctx = _nt_dot(q2, kc) * scale
        o = _na_softmax_pv([s_loc, s_ctx], [vw, vc])
        o_ref[0, pl.ds(q0, GRID_W), :] = jnp.where(low, o[:GRID_W], o[GRID_W:]).astype(BF16)
        return carry

    lax.fori_loop(0, N_ROWS, row_body, 0)

    if need_ctx:
        q2 = _split_heads(qc_ref[0], low)
        o = _na_softmax_pv([_nt_dot(q2, kc) * scale], [vc])
        oc_ref[0] = jnp.where(low, o[:CTX_LEN], o[CTX_LEN:]).astype(BF16)


def _na_bias_table(rpb):
    w = np.arange(GRID_W)
    c0 = np.clip(w - NA_KW // 2, 0, GRID_W - NA_KW)
    kcol = np.arange(GRID_W)
    in_win = (kcol[None, :] >= c0[:, None]) & (kcol[None, :] < c0[:, None] + NA_KW)
    dc = np.clip(kcol[None, :] - w[:, None] + (NA_KW - 1), 0, 2 * NA_KW - 2)
    dr = np.arange(NA_KH)[:, None] + np.arange(NA_KH)[None, :]
    tab = rpb[:, dr[:, :, None, None], dc[None, None, :, :]]
    tab = jnp.where(in_win[None, None, None], tab, NEG_BIG)
    return tab.transpose(0, 1, 3, 2, 4).reshape(NA_HEADS, NA_KH, GRID_W, NA_KH * GRID_W)


def _neighbourhood(proj_l, proj_c, bias, batch, need_ctx):
    n_pairs = NA_HEADS // 2

    def lat(sec):
        return pl.BlockSpec((1, SEQ, LANES), lambda p, b: (sec * NCB + p, b, 0))

    def ctx(sec):
        return pl.BlockSpec((1, CTX_LEN, LANES), lambda p, b: (sec * NCB + p, b, 0))

    in_specs = [lat(S_NQ), lat(S_NK), lat(S_NV), ctx(S_NK), ctx(S_NV),
                pl.BlockSpec((2, NA_KH, GRID_W, NA_KH * GRID_W), lambda p, b: (p, 0, 0, 0))]
    args = [proj_l, proj_l, proj_l, proj_c, proj_c, bias]
    out_specs = [pl.BlockSpec((1, SEQ, LANES), lambda p, b: (p, b, 0))]
    out_shape = [jax.ShapeDtypeStruct((n_pairs, batch * SEQ, LANES), BF16)]
    if need_ctx:
        in_specs.append(ctx(S_NQ))
        args.append(proj_c)
        out_specs.append(pl.BlockSpec((1, CTX_LEN, LANES), lambda p, b: (p, b, 0)))
        out_shape.append(jax.ShapeDtypeStruct((n_pairs, batch * CTX_LEN, LANES), BF16))
    return pl.pallas_call(
        functools.partial(_na_body, need_ctx=need_ctx),
        grid=(n_pairs, batch),
        in_specs=in_specs,
        out_specs=out_specs,
        out_shape=out_shape,
        compiler_params=_params(("arbitrary", "arbitrary")),
        name="neighbourhood_attention",
    )(*args)


def _scan_tile(a, b, h_in, row, reverse):
    for sh in (1, 2, 4):
        if reverse:
            a_s = pltpu.roll(a, SUBLANES - sh, 0)
            b_s = pltpu.roll(b, SUBLANES - sh, 0)
            m = row < SUBLANES - sh
        else:
            a_s = pltpu.roll(a, sh, 0)
            b_s = pltpu.roll(b, sh, 0)
            m = row >= sh
        b = jnp.where(m, a * b_s + b, b)
        a = jnp.where(m, a * a_s, a)
    return b + a * h_in


def _lru_body(*refs, need_ctx):
    if need_ctx:
        (xc_ref, xl_ref, yl_ref, cw_ref, cb_ref, wa_ref, ba_ref, wx_ref, bx_ref, lam_ref, yc_ref,
         o_ref, oc_ref, xpc_ref, xpl_ref, af_ref, bf_ref, ab_ref, bb_ref) = refs
    else:
        (xc_ref, xl_ref, yl_ref, cw_ref, cb_ref, wa_ref, ba_ref, wx_ref, bx_ref, lam_ref,
         o_ref, xpc_ref, xpl_ref, af_ref, bf_ref, ab_ref, bb_ref) = refs
    pad = SUBLANES
    zpad = jnp.zeros((pad, LANES), F32)
    xpc_ref[0:pad, :] = zpad
    xpc_ref[pad:pad + CTX_LEN, :] = xc_ref[0].astype(F32)
    xpc_ref[pad + CTX_LEN:pad + CTX_LEN + pad, :] = zpad
    xpl_ref[0:pad, :] = zpad
    xpl_ref[pad:pad + SEQ, :] = xl_ref[0].astype(F32)
    xpl_ref[pad + SEQ:pad + SEQ + pad, :] = zpad

    cw = cw_ref[...]
    cb = cb_ref[...]
    a_refs = (af_ref, ab_ref)
    b_refs = (bf_ref, bb_ref)
    chunk = CTX_LEN

    def pointwise(xp_ref, t0, dst0):
        taps = [xp_ref[pad + t0 - 1 + j:pad + t0 - 1 + j + chunk, :] for j in range(4)]
        u = cw[0:1] * taps[0] + cw[1:2] * taps[1] + cw[2:3] * taps[2] + cw[3:4] * taps[3] + cb
        ub = u.astype(BF16)
        for d in range(2):
            r = jax.nn.sigmoid(jnp.dot(ub, wa_ref[d, 0], preferred_element_type=F32) + ba_ref[d:d + 1, :])
            i = jax.nn.sigmoid(jnp.dot(ub, wx_ref[d, 0], preferred_element_type=F32) + bx_ref[d:d + 1, :])
            log_a = (-LRU_C) * r * jax.nn.softplus(-lam_ref[d:d + 1, :])
            a_refs[d][dst0:dst0 + chunk, :] = jnp.exp(log_a)
            th = jnp.tanh(log_a)
            b_refs[d][dst0:dst0 + chunk, :] = jnp.sqrt(-2.0 * th / (1.0 - th)) * (i * u)

    pointwise(xpc_ref, 0, 0)
    for cidx in range(SEQ // chunk):
        pointwise(xpl_ref, cidx * chunk, CTX_LEN + cidx * chunk)

    row = lax.broadcasted_iota(jnp.int32, (SUBLANES, LANES), 0)
    n_ctx_tiles = CTX_LEN // SUBLANES
    n_all_tiles = (CTX_LEN + SEQ) // SUBLANES

    def scan_pair(jf, jb, carry):
        hf, hb = carry
        rf = pl.multiple_of(jf * SUBLANES, SUBLANES)
        rb = pl.multiple_of(jb * SUBLANES, SUBLANES)
        tf = _scan_tile(af_ref[pl.ds(rf, SUBLANES), :], bf_ref[pl.ds(rf, SUBLANES), :], hf, row, False)
        tb = _scan_tile(ab_ref[pl.ds(rb, SUBLANES), :], bb_ref[pl.ds(rb, SUBLANES), :], hb, row, True)
        bf_ref[pl.ds(rf, SUBLANES), :] = tf
        bb_ref[pl.ds(rb, SUBLANES), :] = tb
        return tf[SUBLANES - 1:SUBLANES, :], tb[0:1, :]

    h0 = jnp.zeros((1, LANES), F32)
    carry = lax.fori_loop(0, n_ctx_tiles, lambda i, cr: scan_pair(i, n_ctx_tiles - 1 - i, cr), (h0, h0))
    lax.fori_loop(0, n_all_tiles - n_ctx_tiles,
                  lambda i, cr: scan_pair(n_ctx_tiles + i, n_all_tiles - 1 - i, cr), carry)

    def out_body(cidx, _):
        r0 = pl.multiple_of(cidx * chunk, chunk)
        hsum = bf_ref[pl.ds(CTX_LEN + r0, chunk), :] + bb_ref[pl.ds(CTX_LEN + r0, chunk), :]
        o_ref[0, pl.ds(r0, chunk), :] = (hsum * jax.nn.gelu(yl_ref[0, pl.ds(r0, chunk), :].astype(F32))).astype(BF16)
        return 0

    lax.fori_loop(0, SEQ // chunk, out_body, 0)
    if need_ctx:
        hsum = bf_ref[0:CTX_LEN, :] + bb_ref[0:CTX_LEN, :]
        oc_ref[0] = (hsum * jax.nn.gelu(yc_ref[0].astype(F32))).astype(BF16)


def _rglru(proj_l, proj_c, conv_w, conv_b, wa, ba, wx, bx, lam, batch, need_ctx):
    def lat(sec):
        return pl.BlockSpec((1, SEQ, LANES), lambda b, k: (sec * NCB + k, b, 0))

    def ctx(sec):
        return pl.BlockSpec((1, CTX_LEN, LANES), lambda b, k: (sec * NCB + k, b, 0))

    def cols(rows):
        return pl.BlockSpec((rows, LANES), lambda b, k: (0, k))

    gate_w = pl.BlockSpec((2, 1, LANES, LANES), lambda b, k: (0, k, 0, 0))
    n_all = CTX_LEN + SEQ
    in_specs = [ctx(S_LX), lat(S_LX), lat(S_LY), cols(4), cols(1), gate_w, cols(2), gate_w, cols(2), cols(2)]
    args = [proj_c, proj_l, proj_l, conv_w, conv_b.reshape(1, D_MODEL), wa, ba, wx, bx, lam]
    out_specs = [pl.BlockSpec((1, SEQ, LANES), lambda b, k: (k, b, 0))]
    out_shape = [jax.ShapeDtypeStruct((LRU_BLOCKS, batch * SEQ, LANES), BF16)]
    if need_ctx:
        in_specs.append(ctx(S_LY))
        args.append(proj_c)
        out_specs.append(pl.BlockSpec((1, CTX_LEN, LANES), lambda b, k: (k, b, 0)))
        out_shape.append(jax.ShapeDtypeStruct((LRU_BLOCKS, batch * CTX_LEN, LANES), BF16))
    return pl.pallas_call(
        functools.partial(_lru_body, need_ctx=need_ctx),
        grid=(batch, LRU_BLOCKS),
        in_specs=in_specs,
        out_specs=out_specs,
        out_shape=out_shape,
        scratch_shapes=[pltpu.VMEM((CTX_LEN + 2 * SUBLANES, LANES), F32),
                        pltpu.VMEM((SEQ + 2 * SUBLANES, LANES), F32),
                        pltpu.VMEM((n_all, LANES), F32), pltpu.VMEM((n_all, LANES), F32),
                        pltpu.VMEM((n_all, LANES), F32), pltpu.VMEM((n_all, LANES), F32)],
        compiler_params=_params(("arbitrary", "arbitrary")),
        name="rglru",
    )(*args)


def _cat(ref):
    return jnp.concatenate([ref[g] for g in range(NCB)], axis=1)


def _merge_body(x_ref, mod_ref, g2_ref, a_ref, b_ref, c_ref, ga_ref, gb_ref, gc_ref, wb_ref, wo_ref,
                xo_ref, h2_ref):
    oa = jnp.dot(_cat(a_ref), wb_ref[0], preferred_element_type=F32)
    ob = jnp.dot(_cat(b_ref), wb_ref[1], preferred_element_type=F32)
    oc = jnp.dot(_cat(c_ref), wb_ref[2], preferred_element_type=F32)
    mixed = (jax.nn.sigmoid(_cat(ga_ref).astype(F32)) * oa + jax.nn.sigmoid(_cat(gb_ref).astype(F32)) * ob
             + jax.nn.sigmoid(_cat(gc_ref).astype(F32)) * oc)
    y = jnp.dot(mixed.astype(BF16), wo_ref[...], preferred_element_type=F32)
    xn = x_ref[...] + mod_ref[0, 2:3, :] * y
    xo_ref[...] = xn
    ms = jnp.mean(xn * xn, axis=-1, keepdims=True)
    hn = xn * lax.rsqrt(ms + NORM_EPS) * g2_ref[...]
    h2_ref[...] = hn * (1.0 + mod_ref[0, 4:5, :]) + mod_ref[0, 3:4, :]


def _merge(x, mod, mod_row, norm2_g, ret, na, lru, proj, wb, wo, tm):
    t = x.shape[0]

    def cb(sec):
        return pl.BlockSpec((NCB, tm, LANES), lambda i: (sec, i, 0))

    tok = pl.BlockSpec((tm, D_MODEL), lambda i: (i, 0))
    shape = jax.ShapeDtypeStruct((t, D_MODEL), F32)
    return pl.pallas_call(
        _merge_body,
        grid=(t // tm,),
        in_specs=[tok,
                  pl.BlockSpec((1, 6, D_MODEL), lambda i: (mod_row(i), 0, 0)),
                  pl.BlockSpec((1, D_MODEL), lambda i: (0, 0)),
                  cb(0), cb(0), cb(0), cb(S_GA), cb(S_GB), cb(S_GC),
                  pl.BlockSpec((3, D_MODEL, D_MODEL), lambda i: (0, 0, 0)),
                  pl.BlockSpec((D_MODEL, D_MODEL), lambda i: (0, 0))],
        out_specs=[tok, tok],
        out_shape=[shape, shape],
        compiler_params=_params(("arbitrary",)),
        name="merge",
    )(x, mod, norm2_g.reshape(1, D_MODEL), ret, na, lru, proj, proj, proj, wb, wo)


def _router_body(h_ref, wr_ref, br_ref, init_ref, ids_ref, w_ref, within_ref, cnt_ref, run_ref):
    tm = h_ref.shape[0]

    @pl.when(pl.program_id(0) == 0)
    def _():
        run_ref[...] = init_ref[...]

    logits = lax.dot_general(wr_ref[...], h_ref[...], (((1,), (1,)), ((), ())), preferred_element_type=F32,
                             precision=lax.Precision.HIGHEST) + br_ref[...]
    eidx = lax.broadcasted_iota(jnp.int32, (N_EXPERTS, tm), 0).astype(F32)
    cur = logits
    sel = jnp.zeros((N_EXPERTS, tm), F32)
    vals, hots = [], []
    for _ in range(TOP_K):
        m = jnp.max(cur, axis=0, keepdims=True)
        idx = jnp.min(jnp.where(cur == m, eidx, float(N_EXPERTS)), axis=0, keepdims=True)
        hot = eidx == idx
        vals.append(m)
        hots.append(hot)
        sel = sel + hot.astype(F32)
        cur = jnp.where(hot, -jnp.inf, cur)
    exps = [jnp.exp(v - vals[0]) for v in vals]
    den = functools.reduce(jnp.add, exps)

    ti = lax.broadcasted_iota(jnp.int32, (tm, tm), 0)
    tj = lax.broadcasted_iota(jnp.int32, (tm, tm), 1)
    upper = (ti < tj).astype(BF16)
    before = jnp.dot(sel.astype(BF16), upper, preferred_element_type=F32) + run_ref[:, 0:1]
    for k in range(TOP_K):
        ids_ref[k:k + 1, :] = jnp.sum(jnp.where(hots[k], eidx, 0.0), axis=0, keepdims=True).astype(jnp.int32)
        w_ref[k:k + 1, :] = exps[k] / den
        within_ref[k:k + 1, :] = jnp.sum(jnp.where(hots[k], before, 0.0), axis=0, keepdims=True).astype(jnp.int32)
    run_ref[...] = run_ref[...] + jnp.sum(sel, axis=1, keepdims=True)
    cnt_ref[...] = run_ref[...]


def _router(h2, w_router_t, b_router, init_counts, tm):
    t = h2.shape[0]
    tok4 = pl.BlockSpec((TOP_K, tm), lambda i: (0, i))
    cnt = pl.BlockSpec((N_EXPERTS, LANES), lambda i: (0, 0))
    return pl.pallas_call(
        _router_body,
        grid=(t // tm,),
        in_specs=[pl.BlockSpec((tm, D_MODEL), lambda i: (i, 0)),
                  pl.BlockSpec((N_EXPERTS, D_MODEL), lambda i: (0, 0)),
                  pl.BlockSpec((N_EXPERTS, 1), lambda i: (0, 0)),
                  cnt],
        out_specs=[tok4, tok4, tok4, cnt],
        out_shape=[jax.ShapeDtypeStruct((TOP_K, t), jnp.int32), jax.ShapeDtypeStruct((TOP_K, t), F32),
                   jax.ShapeDtypeStruct((TOP_K, t), jnp.int32),
                   jax.ShapeDtypeStruct((N_EXPERTS, LANES), F32)],
        scratch_shapes=[pltpu.VMEM((N_EXPERTS, LANES), F32)],
        compiler_params=_params(("arbitrary",)),
        name="router_topk",
    )(h2, w_router_t, b_router.reshape(N_EXPERTS, 1), init_counts)


def _row_copy(src_ref, src_row, dst_ref, dst_row, sem):
    return pltpu.make_async_copy(src_ref.at[pl.ds(pl.multiple_of(src_row * SUBLANES, SUBLANES), SUBLANES)],
                                 dst_ref.at[pl.ds(pl.multiple_of(dst_row * SUBLANES, SUBLANES), SUBLANES)], sem)


def _dispatch_body(pos_ref, h_ref, xg_in_ref, xg_ref, slab_ref, sem):
    del xg_in_ref
    tm = h_ref.shape[0]
    for s in range(NCB):
        slab_ref[pl.ds(s, tm, stride=SUBLANES), :] = h_ref[:, s * LANES:(s + 1) * LANES]

    def start(t, _):
        for k in range(TOP_K):
            _row_copy(slab_ref, t, xg_ref, pos_ref[k, t], sem).start()
        return 0

    lax.fori_loop(0, tm, start, 0)

    def wait(t, _):
        for k in range(TOP_K):
            _row_copy(slab_ref, t, xg_ref, pos_ref[k, t], sem).wait()
        return 0

    lax.fori_loop(0, tm, wait, 0)


def _dispatch(h2, pos, xg, tm):
    t = h2.shape[0]
    return pl.pallas_call(
        _dispatch_body,
        grid=(t // tm,),
        in_specs=[pl.BlockSpec((TOP_K, tm), lambda i: (0, i), memory_space=pltpu.SMEM),
                  pl.BlockSpec((tm, D_MODEL), lambda i: (i, 0)),
                  pl.BlockSpec(memory_space=pl.ANY)],
        out_specs=pl.BlockSpec(memory_space=pl.ANY),
        out_shape=jax.ShapeDtypeStruct(xg.shape, F32),
        scratch_shapes=[pltpu.VMEM((tm * SUBLANES, LANES), F32), pltpu.SemaphoreType.DMA],
        input_output_aliases={2: 0},
        compiler_params=_params(("arbitrary",)),
        name="moe_dispatch",
    )(pos, h2, xg)


def _expert_body(te_ref, nu_ref, x_ref, wg_ref, bg_ref, wl_ref, bl_ref, wd_ref, bd_ref, y_ref):
    del te_ref
    tm = x_ref.shape[0] // SUBLANES

    @pl.when(pl.program_id(0) < nu_ref[0])
    def _():
        x = jnp.concatenate([x_ref[pl.ds(s, tm, stride=SUBLANES), :] for s in range(NCB)], axis=1).astype(BF16)
        glu = jnp.minimum(jnp.dot(x, wg_ref[0], preferred_element_type=F32) + bg_ref[0], SWIGLU_LIMIT)
        lin = jnp.clip(jnp.dot(x, wl_ref[0], preferred_element_type=F32) + bl_ref[0], -SWIGLU_LIMIT, SWIGLU_LIMIT)
        act = glu * jax.nn.sigmoid(SWIGLU_ALPHA * glu) * (lin + 1.0)
        y = jnp.dot(act.astype(BF16), wd_ref[0], preferred_element_type=F32) + bd_ref[0]
        for s in range(NCB):
            y_ref[pl.ds(s, tm, stride=SUBLANES), :] = y[:, s * LANES:(s + 1) * LANES]


def _experts(xg, tile_expert, n_used, wg, bg, wl, bl, wd, bd):
    tm = EXPERT_TILE
    n_tiles = xg.shape[0] // (tm * SUBLANES)

    def rows(m, te, nu):
        return (jnp.minimum(m, nu[0] - 1), 0)

    def per_expert(shape):
        return pl.BlockSpec((1,) + shape, lambda m, te, nu: (te[m], 0, 0))

    grid_spec = pltpu.PrefetchScalarGridSpec(
        num_scalar_prefetch=2,
        grid=(n_tiles,),
        in_specs=[pl.BlockSpec((tm * SUBLANES, LANES), rows),
                  per_expert((D_MODEL, D_MODEL)), per_expert((1, D_MODEL)),
                  per_expert((D_MODEL, D_MODEL)), per_expert((1, D_MODEL)),
                  per_expert((D_MODEL, D_MODEL)), per_expert((1, D_MODEL))],
        out_specs=pl.BlockSpec((tm * SUBLANES, LANES), rows),
    )
    return pl.pallas_call(
        _expert_body,
        grid_spec=grid_spec,
        out_shape=jax.ShapeDtypeStruct(xg.shape, F32),
        compiler_params=_params(("arbitrary",)),
        name="moe_experts",
    )(tile_expert, n_used, xg, wg, bg, wl, bl, wd, bd)


def _combine_body(pos_ref, x_ref, w_ref, mod_ref, fg_ref, yg_ref, o_ref, ybuf_ref, sem, *, final_norm):
    tm = x_ref.shape[0]

    def start(t, _):
        for k in range(TOP_K):
            _row_copy(yg_ref, pos_ref[k, t], ybuf_ref.at[k], t, sem).start()
        return 0

    lax.fori_loop(0, tm, start, 0)

    def wait(t, _):
        for k in range(TOP_K):
            _row_copy(yg_ref, pos_ref[k, t], ybuf_ref.at[k], t, sem).wait()
        return 0

    lax.fori_loop(0, tm, wait, 0)

    acc = jnp.zeros((tm, D_MODEL), F32)
    for k in range(TOP_K):
        yk = jnp.concatenate([ybuf_ref[k, pl.ds(s, tm, stride=SUBLANES), :] for s in range(NCB)], axis=1)
        acc = acc + w_ref[:, k:k + 1] * yk
    xn = x_ref[...] + mod_ref[0, 5:6, :] * acc
    if final_norm:
        ms = jnp.mean(xn * xn, axis=-1, keepdims=True)
        xn = xn * lax.rsqrt(ms + NORM_EPS) * fg_ref[...]
    o_ref[...] = xn


def _combine(x, w_tok, pos, yg, mod, mod_row, final_g, tm, final_norm):
    t = x.shape[0]
    return pl.pallas_call(
        functools.partial(_combine_body, final_norm=final_norm),
        grid=(t // tm,),
        in_specs=[pl.BlockSpec((TOP_K, tm), lambda i: (0, i), memory_space=pltpu.SMEM),
                  pl.BlockSpec((tm, D_MODEL), lambda i: (i, 0)),
                  pl.BlockSpec((tm, TOP_K), lambda i: (i, 0)),
                  pl.BlockSpec((1, 6, D_MODEL), lambda i: (mod_row(i), 0, 0)),
                  pl.BlockSpec((1, D_MODEL), lambda i: (0, 0)),
                  pl.BlockSpec(memory_space=pl.ANY)],
        out_specs=pl.BlockSpec((tm, D_MODEL), lambda i: (i, 0)),
        out_shape=jax.ShapeDtypeStruct((t, D_MODEL), F32),
        scratch_shapes=[pltpu.VMEM((TOP_K, tm * SUBLANES, LANES), F32), pltpu.SemaphoreType.DMA],
        compiler_params=_params(("arbitrary",)),
        name="moe_combine",
    )(pos, x, w_tok, mod, final_g.reshape(1, D_MODEL), yg)


def _moe(streams, mod, w_router, b_router, w_up, b_up, w_down, b_down, final_g, final_norm):
    te = EXPERT_TILE
    w_router_t = w_router.T
    counts = jnp.zeros((N_EXPERTS, LANES), F32)
    routed = []
    for _, h2, _ in streams:
        ids, w, within, counts = _router(h2, w_router_t, b_router, counts, min(512, h2.shape[0]))
        routed.append((ids, w, within))
    total = sum(h2.shape[0] for _, h2, _ in streams)
    cnt = counts[:, 0].astype(jnp.int32)
    padded = ((cnt + te - 1) // te) * te
    ends = jnp.cumsum(padded)
    starts = ends - padded
    n_tiles = total * TOP_K // te + N_EXPERTS
    tile_expert = jnp.minimum(jnp.searchsorted(ends // te, jnp.arange(n_tiles, dtype=jnp.int32), side="right"),
                              N_EXPERTS - 1).astype(jnp.int32)
    n_used = (ends[-1:] // te).astype(jnp.int32)
    expert_ids = jnp.arange(N_EXPERTS, dtype=jnp.int32)[:, None, None]

    wg = w_up[:, :, 0::2].astype(BF16)
    wl = w_up[:, :, 1::2].astype(BF16)
    bg = b_up[:, None, 0::2]
    bl = b_up[:, None, 1::2]

    xg = jnp.zeros((n_tiles * te * SUBLANES, LANES), F32)
    positions = []
    for (_, h2, _), (ids, _, within) in zip(streams, routed):
        pos = within + jnp.sum(jnp.where(ids[None] == expert_ids, starts[:, None, None], 0), axis=0)
        positions.append(pos)
        xg = _dispatch(h2, pos, xg, ROW_DMA_TILE)
    yg = _experts(xg, tile_expert, n_used, wg, bg, wl, bl, w_down.astype(BF16), b_down[:, None, :])
    return [_combine(x, w.T, pos, yg, mod, mod_row, final_g, ROW_DMA_TILE, final_norm)
            for (x, _, mod_row), (_, w, _), pos in zip(streams, routed, positions)]


def _rope_tables():
    tok = jnp.arange(SEQ)
    nf = 32
    inv = ROPE_BASE ** (-jnp.arange(nf, dtype=F32) / nf)
    ang_r = (tok // GRID_W).astype(F32)[:, None] * inv
    ang_c = (tok % GRID_W).astype(F32)[:, None] * inv
    cos = jnp.concatenate([jnp.cos(ang_r)] * 2 + [jnp.cos(ang_c)] * 2, axis=-1)
    sin = jnp.concatenate([-jnp.sin(ang_r), jnp.sin(ang_r), -jnp.sin(ang_c), jnp.sin(ang_c)], axis=-1)
    return cos, sin


def kernel(x, c, ctx, c_ctx, w_mod, b_mod, norm1_g, norm2_g, w_mix_in, ret_decay_fwd, ret_decay_bwd, na_rel_bias, lru_conv_w, lru_conv_b, lru_gate_a_w, lru_gate_a_b, lru_gate_x_w, lru_gate_x_b, lru_lambda, w_branch, w_mix_out, w_router, b_router, w_expert_up, b_expert_up, w_expert_down, b_expert_down, final_norm_g):
    batch = x.shape[0]
    depth = w_mod.shape[0]
    assert x.shape[1:] == (SEQ, D_MODEL) and ctx.shape[1:] == (CTX_LEN, D_MODEL)
    tm_l = TOKEN_TILE
    tm_c = min(TOKEN_TILE, batch * CTX_LEN)
    tm_ml = MERGE_TILE
    tm_mc = min(MERGE_TILE, batch * CTX_LEN)
    assert SEQ % tm_l == 0 and (batch * CTX_LEN) % tm_c == 0 and (batch * CTX_LEN) % tm_mc == 0

    def lat_row(tile_rows):
        return lambda i: i // (SEQ // tile_rows)

    def ctx_row(tile_rows):
        del tile_rows
        return lambda i: batch

    xl = x.reshape(batch * SEQ, D_MODEL)
    xc = ctx.reshape(batch * CTX_LEN, D_MODEL)
    mp = -(-(batch + 1) // SUBLANES) * SUBLANES
    cc = jnp.zeros((mp, D_MODEL), F32).at[:batch].set(c).at[batch].set(c_ctx)
    cos, sin_signed = _rope_tables()

    for l in range(depth):
        last = l == depth - 1
        need_ctx = not last
        mod = _modulation(cc, w_mod[l], b_mod[l])
        w_in = w_mix_in[l].astype(BF16)
        proj_l = _input_projection(xl, mod, lat_row(tm_l), norm1_g[l], w_in, tm_l)
        proj_c = _input_projection(xc, mod, ctx_row(tm_c), norm1_g[l], w_in, tm_c)
        lg = jnp.stack([jax.nn.log_sigmoid(ret_decay_fwd[l].astype(F32)),
                        jax.nn.log_sigmoid(ret_decay_bwd[l].astype(F32))])
        ret = _retention(proj_l, proj_c, lg, cos, sin_signed, batch, need_ctx)
        na = _neighbourhood(proj_l, proj_c, _na_bias_table(na_rel_bias[l]), batch, need_ctx)
        lru = _rglru(proj_l, proj_c, lru_conv_w[l], lru_conv_b[l], lru_gate_a_w[l].astype(BF16), lru_gate_a_b[l],
                     lru_gate_x_w[l].astype(BF16), lru_gate_x_b[l], lru_lambda[l], batch, need_ctx)
        wb = w_branch[l].astype(BF16)
        wo = w_mix_out[l].astype(BF16)
        xl, h2l = _merge(xl, mod, lat_row(tm_ml), norm2_g[l], ret[0], na[0], lru[0], proj_l, wb, wo, tm_ml)
        streams = [(xl, h2l, lat_row(ROW_DMA_TILE))]
        if need_ctx:
            xc, h2c = _merge(xc, mod, ctx_row(tm_mc), norm2_g[l], ret[1], na[1], lru[1], proj_c, wb, wo, tm_mc)
            streams = [(xc, h2c, ctx_row(ROW_DMA_TILE))] + streams
        outs = _moe(streams, mod, w_router[l], b_router[l], w_expert_up[l], b_expert_up[l], w_expert_down[l],
                    b_expert_down[l], final_norm_g, last)
        xl = outs[-1]
        if need_ctx:
            xc = outs[0]
    return xl.reshape(batch, SEQ, D_MODEL)
```

```python
import functools

import numpy as np
import jax
import jax.numpy as jnp
from jax import lax
from jax.experimental import pallas as pl
from jax.experimental.pallas import tpu as pltpu

F32 = jnp.float32
BF16 = jnp.bfloat16

D_MODEL = 1024
SEQ = 2048
CTX_LEN = 256
GRID_W = 64
N_ROWS = SEQ // GRID_W
LANES = 128
SUBLANES = 8
NCB = D_MODEL // LANES
N_SEC = 12
RET_HEADS = 8
RET_CHUNK = 256
NA_HEADS = 16
NA_KH = 8
NA_KW = 16
NA_ROWS_PER_STEP = 4
LRU_SCAN_TILES = 4
LRU_BLOCKS = 8
LRU_C = 8.0
N_EXPERTS = 32
TOP_K = 4
SWIGLU_LIMIT = 7.0
SWIGLU_ALPHA = 1.702
ROPE_BASE = 10000.0
NORM_EPS = 1e-6
NEG_BIG = -1e30
TOKEN_TILE = 1024
MERGE_TILE = 512
ROW_DMA_TILE = 256
EXPERT_TILE = 256
VMEM_LIMIT = 56 * 1024 * 1024

S_RQ, S_RK, S_RV, S_RG, S_NQ, S_NK, S_NV, S_LX, S_LY, S_GA, S_GB, S_GC = range(12)


def _nt_dot(a, b):
    return lax.dot_general(a, b, (((1,), (1,)), ((), ())), preferred_element_type=F32)


def _tn_dot(a, b):
    return lax.dot_general(a, b, (((0,), (0,)), ((), ())), preferred_element_type=F32)


def _params(sem, vmem=VMEM_LIMIT):
    return pltpu.CompilerParams(dimension_semantics=sem, vmem_limit_bytes=vmem)


def _mod_body(c_ref, w_ref, b_ref, o_ref):
    c = c_ref[...]
    s = c * jax.nn.sigmoid(c)
    o_ref[...] = jnp.dot(s, w_ref[...], preferred_element_type=F32,
                         precision=lax.Precision.HIGHEST) + b_ref[...]


def _modulation(cc, w_mod, b_mod):
    mp = cc.shape[0]
    out = pl.pallas_call(
        _mod_body,
        grid=(6,),
        in_specs=[pl.BlockSpec((mp, D_MODEL), lambda j: (0, 0)),
                  pl.BlockSpec((D_MODEL, D_MODEL), lambda j: (0, j)),
                  pl.BlockSpec((1, D_MODEL), lambda j: (0, j))],
        out_specs=pl.BlockSpec((mp, D_MODEL), lambda j: (0, j)),
        out_shape=jax.ShapeDtypeStruct((mp, 6 * D_MODEL), F32),
        compiler_params=_params(("arbitrary",)),
        name="adaln_mod",
    )(cc, w_mod, b_mod.reshape(1, 6 * D_MODEL))
    return out.reshape(mp, 6, D_MODEL)


def _inproj_body(x_ref, mod_ref, g_ref, w_ref, o_ref, h_ref):
    @pl.when(pl.program_id(1) == 0)
    def _():
        x = x_ref[...]
        ms = jnp.mean(x * x, axis=-1, keepdims=True)
        y = x * lax.rsqrt(ms + NORM_EPS) * g_ref[...]
        h_ref[...] = (y * (1.0 + mod_ref[0, 1:2, :]) + mod_ref[0, 0:1, :]).astype(BF16)

    acc = jnp.dot(h_ref[...], w_ref[...], preferred_element_type=F32)
    for g in range(NCB):
        o_ref[g] = acc[:, g * LANES:(g + 1) * LANES].astype(BF16)


def _input_projection(x, mod, mod_row, norm_g, w_in, tm):
    t = x.shape[0]
    return pl.pallas_call(
        _inproj_body,
        grid=(t // tm, N_SEC),
        in_specs=[pl.BlockSpec((tm, D_MODEL), lambda i, j: (i, 0)),
                  pl.BlockSpec((1, 6, D_MODEL), lambda i, j: (mod_row(i), 0, 0)),
                  pl.BlockSpec((1, D_MODEL), lambda i, j: (0, 0)),
                  pl.BlockSpec((D_MODEL, D_MODEL), lambda i, j: (0, j))],
        out_specs=pl.BlockSpec((NCB, tm, LANES), lambda i, j: (j, i, 0)),
        out_shape=jax.ShapeDtypeStruct((N_SEC * NCB, t, LANES), BF16),
        scratch_shapes=[pltpu.VMEM((tm, D_MODEL), BF16)],
        compiler_params=_params(("arbitrary", "arbitrary")),
        name="in_proj",
    )(x, mod, norm_g.reshape(1, D_MODEL), w_in)


def _rope(x, cos, sin_signed, first_half):
    partner = jnp.where(first_half, pltpu.roll(x, LANES - 32, 1), pltpu.roll(x, 32, 1))
    return x * cos + partner * sin_signed


def _head_rms_gate(o, gate):
    o = o * lax.rsqrt(jnp.mean(o * o, axis=-1, keepdims=True) + NORM_EPS)
    return o * (gate * jax.nn.sigmoid(gate))


def _ret_body(*refs, need_ctx):
    if need_ctx:
        (lg_ref, q_ref, k_ref, v_ref, rg_ref, kc_ref, vc_ref, cos_ref, sin_ref, qc_ref, rgc_ref,
         o_ref, oc_ref, kr_ref, sb_ref, dc_ref) = refs
    else:
        (lg_ref, q_ref, k_ref, v_ref, rg_ref, kc_ref, vc_ref, cos_ref, sin_ref,
         o_ref, kr_ref, sb_ref, dc_ref) = refs
    h = pl.program_id(1)
    c = RET_CHUNK
    n_chunks = SEQ // c
    lgf = lg_ref[0, h]
    lgb = lg_ref[1, h]
    ks = float(LANES) ** -0.5

    ia = lax.broadcasted_iota(jnp.int32, (c, 1), 0).astype(F32)
    ii = lax.broadcasted_iota(jnp.int32, (c, c), 0)
    jj = lax.broadcasted_iota(jnp.int32, (c, c), 1)
    diff = (ii - jj).astype(F32)
    dc_ref[...] = jnp.where(diff >= 0, jnp.exp(diff * lgf), jnp.exp(-diff * lgb))
    w_f = jnp.exp((c - 1.0 - ia) * lgf)
    w_b = jnp.exp(ia * lgb)
    x_f = jnp.exp((ia + 1.0) * lgf)
    x_b = jnp.exp((c - ia) * lgb)
    g_f = jnp.exp(c * lgf)
    g_b = jnp.exp(c * lgb)
    lane = lax.broadcasted_iota(jnp.int32, (1, LANES), 1)
    first_half = (lane % 64) < 32

    kc = kc_ref[0].astype(F32)
    vc = vc_ref[0]
    sf0 = _tn_dot((kc * w_f).astype(BF16), vc)
    sb0 = _tn_dot((kc * w_b).astype(BF16), vc)

    def bwd_body(i, sb):
        n = n_chunks - 1 - i
        r0 = pl.multiple_of(n * c, c)
        sb_ref[n] = sb
        k = _rope(k_ref[0, pl.ds(r0, c), :].astype(F32), cos_ref[pl.ds(r0, c), :], sin_ref[pl.ds(r0, c), :],
                  first_half)
        kr_ref[pl.ds(r0, c), :] = k.astype(BF16)
        ub = _tn_dot((k * w_b).astype(BF16), v_ref[0, pl.ds(r0, c), :])
        return ub + g_b * sb

    lax.fori_loop(0, n_chunks, bwd_body, sb0, unroll=True)

    def fwd_body(n, sf):
        r0 = pl.multiple_of(n * c, c)
        q = _rope(q_ref[0, pl.ds(r0, c), :].astype(F32), cos_ref[pl.ds(r0, c), :], sin_ref[pl.ds(r0, c), :],
                  first_half)
        k = kr_ref[pl.ds(r0, c), :]
        v = v_ref[0, pl.ds(r0, c), :]
        s = _nt_dot(q.astype(BF16), k)
        o = jnp.dot((s * dc_ref[...]).astype(BF16), v, preferred_element_type=F32)
        o = o + jnp.dot((q * x_f).astype(BF16), sf.astype(BF16), preferred_element_type=F32)
        o = o + jnp.dot((q * x_b).astype(BF16), sb_ref[n].astype(BF16), preferred_element_type=F32)
        o = _head_rms_gate(o * ks, rg_ref[0, pl.ds(r0, c), :].astype(F32))
        o_ref[0, pl.ds(r0, c), :] = o.astype(BF16)
        uf = _tn_dot((k.astype(F32) * w_f).astype(BF16), v)
        return uf + g_f * sf

    lax.fori_loop(0, n_chunks, fwd_body, sf0, unroll=True)

    if need_ctx:
        s = _nt_dot(qc_ref[0], kc_ref[0])
        o = jnp.dot((s * dc_ref[...]).astype(BF16), vc, preferred_element_type=F32)
        oc_ref[0] = _head_rms_gate(o * ks, rgc_ref[0].astype(F32)).astype(BF16)


def _retention(proj_l, proj_c, lg, cos, sin_signed, batch, need_ctx):
    def lat(sec):
        return pl.BlockSpec((1, SEQ, LANES), lambda b, h: (sec * NCB + h, b, 0))

    def ctx(sec):
        return pl.BlockSpec((1, CTX_LEN, LANES), lambda b, h: (sec * NCB + h, b, 0))

    table = pl.BlockSpec((SEQ, LANES), lambda b, h: (0, 0))
    in_specs = [pl.BlockSpec(memory_space=pltpu.SMEM),
                lat(S_RQ), lat(S_RK), lat(S_RV), lat(S_RG), ctx(S_RK), ctx(S_RV), table, table]
    args = [lg, proj_l, proj_l, proj_l, proj_l, proj_c, proj_c, cos, sin_signed]
    out_specs = [pl.BlockSpec((1, SEQ, LANES), lambda b, h: (h, b, 0))]
    out_shape = [jax.ShapeDtypeStruct((RET_HEADS, batch * SEQ, LANES), BF16)]
    if need_ctx:
        in_specs += [ctx(S_RQ), ctx(S_RG)]
        args += [proj_c, proj_c]
        out_specs.append(pl.BlockSpec((1, CTX_LEN, LANES), lambda b, h: (h, b, 0)))
        out_shape.append(jax.ShapeDtypeStruct((RET_HEADS, batch * CTX_LEN, LANES), BF16))
    return pl.pallas_call(
        functools.partial(_ret_body, need_ctx=need_ctx),
        grid=(batch, RET_HEADS),
        in_specs=in_specs,
        out_specs=out_specs,
        out_shape=out_shape,
        scratch_shapes=[pltpu.VMEM((SEQ, LANES), BF16),
                        pltpu.VMEM((SEQ // RET_CHUNK, LANES, LANES), F32),
                        pltpu.VMEM((RET_CHUNK, RET_CHUNK), F32)],
        compiler_params=_params(("arbitrary", "arbitrary")),
        name="retention",
    )(*args)


def _na_softmax_pv(s_parts, v_parts):
    m = functools.reduce(jnp.maximum, [jnp.max(s, axis=-1, keepdims=True) for s in s_parts])
    p_parts = [jnp.exp(s - m) for s in s_parts]
    den = functools.reduce(jnp.add, [jnp.sum(p, axis=-1, keepdims=True) for p in p_parts])
    o = functools.reduce(jnp.add, [jnp.dot(p.astype(BF16), v, preferred_element_type=F32)
                                   for p, v in zip(p_parts, v_parts)])
    return o / den


def _split_heads(q, low):
    zero = jnp.zeros_like(q)
    return jnp.concatenate([jnp.where(low, q, zero), jnp.where(low, zero, q)], axis=0)


def _na_body(*refs, need_ctx):
    if need_ctx:
        q_ref, k_ref, v_ref, kc_ref, vc_ref, bias_ref, qc_ref, o_ref, oc_ref = refs
    else:
        q_ref, k_ref, v_ref, kc_ref, vc_ref, bias_ref, o_ref = refs
    lane = lax.broadcasted_iota(jnp.int32, (1, LANES), 1)
    low = lane < 64
    scale = 64.0 ** -0.5
    kc = kc_ref[0]
    vc = vc_ref[0]
    win = NA_KH * GRID_W

    def one_row(r):
        r0 = jnp.clip(r - NA_KH // 2, 0, N_ROWS - NA_KH)
        off = r0 - r + (NA_KH - 1)
        q0 = pl.multiple_of(r * GRID_W, GRID_W)
        k0 = pl.multiple_of(r0 * GRID_W, GRID_W)
        q2 = _split_heads(q_ref[0, pl.ds(q0, GRID_W), :] * scale, low)
        kw = k_ref[0, pl.ds(k0, win), :]
        vw = v_ref[0, pl.ds(k0, win), :]
        bias = jnp.concatenate([bias_ref[0, off], bias_ref[1, off]], axis=0)
        o = _na_softmax_pv([_nt_dot(q2, kw) + bias, _nt_dot(q2, kc)], [vw, vc])
        o_ref[0, pl.ds(q0, GRID_W), :] = jnp.where(low, o[:GRID_W], o[GRID_W:]).astype(BF16)

    def row_body(i, carry):
        for j in range(NA_ROWS_PER_STEP):
            one_row(i * NA_ROWS_PER_STEP + j)
        return carry

    lax.fori_loop(0, N_ROWS // NA_ROWS_PER_STEP, row_body, 0)

    if need_ctx:
        q2 = _split_heads(qc_ref[0] * scale, low)
        o = _na_softmax_pv([_nt_dot(q2, kc)], [vc])
        oc_ref[0] = jnp.where(low, o[:CTX_LEN], o[CTX_LEN:]).astype(BF16)


def _na_bias_table(rpb):
    w = np.arange(GRID_W)
    c0 = np.clip(w - NA_KW // 2, 0, GRID_W - NA_KW)
    kcol = np.arange(GRID_W)
    in_win = (kcol[None, :] >= c0[:, None]) & (kcol[None, :] < c0[:, None] + NA_KW)
    dc = kcol[None, :] - w[:, None] + (NA_KW - 1)
    pick = ((dc[None] == np.arange(2 * NA_KW - 1)[:, None, None]) & in_win[None]).astype(np.float32)
    band = jnp.einsum("hrc,cwk->hrwk", rpb.astype(F32), pick, precision=lax.Precision.HIGHEST)
    band = jnp.where(in_win[None, None], band, NEG_BIG)
    tab = jnp.stack([band[:, off:off + NA_KH] for off in range(NA_KH)], axis=1)
    return tab.transpose(0, 1, 3, 2, 4).reshape(NA_HEADS, NA_KH, GRID_W, NA_KH * GRID_W)


def _neighbourhood(proj_l, proj_c, bias, batch, need_ctx):
    n_pairs = NA_HEADS // 2

    def lat(sec):
        return pl.BlockSpec((1, SEQ, LANES), lambda p, b: (sec * NCB + p, b, 0))

    def ctx(sec):
        return pl.BlockSpec((1, CTX_LEN, LANES), lambda p, b: (sec * NCB + p, b, 0))

    in_specs = [lat(S_NQ), lat(S_NK), lat(S_NV), ctx(S_NK), ctx(S_NV),
                pl.BlockSpec((2, NA_KH, GRID_W, NA_KH * GRID_W), lambda p, b: (p, 0, 0, 0))]
    args = [proj_l, proj_l, proj_l, proj_c, proj_c, bias]
    out_specs = [pl.BlockSpec((1, SEQ, LANES), lambda p, b: (p, b, 0))]
    out_shape = [jax.ShapeDtypeStruct((n_pairs, batch * SEQ, LANES), BF16)]
    if need_ctx:
        in_specs.append(ctx(S_NQ))
        args.append(proj_c)
        out_specs.append(pl.BlockSpec((1, CTX_LEN, LANES), lambda p, b: (p, b, 0)))
        out_shape.append(jax.ShapeDtypeStruct((n_pairs, batch * CTX_LEN, LANES), BF16))
    return pl.pallas_call(
        functools.partial(_na_body, need_ctx=need_ctx),
        grid=(n_pairs, batch),
        in_specs=in_specs,
        out_specs=out_specs,
        out_shape=out_shape,
        compiler_params=_params(("arbitrary", "arbitrary")),
        name="neighbourhood_attention",
    )(*args)


def _scan_tile(a, b, row, reverse):
    for sh in (1, 2, 4):
        if reverse:
            a_s = pltpu.roll(a, SUBLANES - sh, 0)
            b_s = pltpu.roll(b, SUBLANES - sh, 0)
            m = row < SUBLANES - sh
        else:
            a_s = pltpu.roll(a, sh, 0)
            b_s = pltpu.roll(b, sh, 0)
            m = row >= sh
        b = jnp.where(m, a * b_s + b, b)
        a = jnp.where(m, a * a_s, a)
    return a, b


def _lru_body(*refs, need_ctx):
    if need_ctx:
        (xc_ref, xl_ref, yl_ref, cw_ref, cb_ref, wa_ref, ba_ref, wx_ref, bx_ref, lam_ref, yc_ref,
         o_ref, oc_ref, xpc_ref, xpl_ref, af_ref, bf_ref, ab_ref, bb_ref) = refs
    else:
        (xc_ref, xl_ref, yl_ref, cw_ref, cb_ref, wa_ref, ba_ref, wx_ref, bx_ref, lam_ref,
         o_ref, xpc_ref, xpl_ref, af_ref, bf_ref, ab_ref, bb_ref) = refs
    pad = SUBLANES
    zpad = jnp.zeros((pad, LANES), F32)
    xpc_ref[0:pad, :] = zpad
    xpc_ref[pad:pad + CTX_LEN, :] = xc_ref[0].astype(F32)
    xpc_ref[pad + CTX_LEN:pad + CTX_LEN + pad, :] = zpad
    xpl_ref[0:pad, :] = zpad
    xpl_ref[pad:pad + SEQ, :] = xl_ref[0].astype(F32)
    xpl_ref[pad + SEQ:pad + SEQ + pad, :] = zpad

    cw = cw_ref[...]
    cb = cb_ref[...]
    a_refs = (af_ref, ab_ref)
    b_refs = (bf_ref, bb_ref)
    chunk = CTX_LEN

    def pointwise(xp_ref, t0, dst0):
        taps = [xp_ref[pad + t0 - 1 + j:pad + t0 - 1 + j + chunk, :] for j in range(4)]
        u = cw[0:1] * taps[0] + cw[1:2] * taps[1] + cw[2:3] * taps[2] + cw[3:4] * taps[3] + cb
        ub = u.astype(BF16)
        for d in range(2):
            r = jax.nn.sigmoid(jnp.dot(ub, wa_ref[d, 0], preferred_element_type=F32) + ba_ref[d:d + 1, :])
            i = jax.nn.sigmoid(jnp.dot(ub, wx_ref[d, 0], preferred_element_type=F32) + bx_ref[d:d + 1, :])
            log_a = (-LRU_C) * r * jax.nn.softplus(-lam_ref[d:d + 1, :])
            a_refs[d][dst0:dst0 + chunk, :] = jnp.exp(log_a)
            th = jnp.tanh(log_a)
            b_refs[d][dst0:dst0 + chunk, :] = jnp.sqrt(-2.0 * th / (1.0 - th)) * (i * u)

    pointwise(xpc_ref, 0, 0)
    for cidx in range(SEQ // chunk):
        pointwise(xpl_ref, cidx * chunk, CTX_LEN + cidx * chunk)

    row = lax.broadcasted_iota(jnp.int32, (SUBLANES, LANES), 0)
    n_ctx_tiles = CTX_LEN // SUBLANES
    n_all_tiles = (CTX_LEN + SEQ) // SUBLANES

    def scan_group(jf, jb, carry):
        hf, hb = carry
        rows_f = [pl.multiple_of((jf + g) * SUBLANES, SUBLANES) for g in range(LRU_SCAN_TILES)]
        rows_b = [pl.multiple_of((jb - g) * SUBLANES, SUBLANES) for g in range(LRU_SCAN_TILES)]
        scans_f = [_scan_tile(af_ref[pl.ds(r, SUBLANES), :], bf_ref[pl.ds(r, SUBLANES), :], row, False)
                   for r in rows_f]
        scans_b = [_scan_tile(ab_ref[pl.ds(r, SUBLANES), :], bb_ref[pl.ds(r, SUBLANES), :], row, True)
                   for r in rows_b]
        out_f, out_b = [], []
        for (af, bf), (ab, bb) in zip(scans_f, scans_b):
            tf = bf + af * hf
            tb = bb + ab * hb
            out_f.append(tf)
            out_b.append(tb)
            hf, hb = tf[SUBLANES - 1:SUBLANES, :], tb[0:1, :]
        for r, t in zip(rows_f, out_f):
            bf_ref[pl.ds(r, SUBLANES), :] = t
        for r, t in zip(rows_b, out_b):
            bb_ref[pl.ds(r, SUBLANES), :] = t
        return hf, hb

    g = LRU_SCAN_TILES
    h0 = jnp.zeros((1, LANES), F32)
    carry = lax.fori_loop(0, n_ctx_tiles // g,
                          lambda i, cr: scan_group(i * g, n_ctx_tiles - 1 - i * g, cr), (h0, h0))
    lax.fori_loop(0, (n_all_tiles - n_ctx_tiles) // g,
                  lambda i, cr: scan_group(n_ctx_tiles + i * g, n_all_tiles - 1 - i * g, cr), carry)

    def out_body(cidx, _):
        r0 = pl.multiple_of(cidx * chunk, chunk)
        hsum = bf_ref[pl.ds(CTX_LEN + r0, chunk), :] + bb_ref[pl.ds(CTX_LEN + r0, chunk), :]
        o_ref[0, pl.ds(r0, chunk), :] = (hsum * jax.nn.gelu(yl_ref[0, pl.ds(r0, chunk), :].astype(F32))).astype(BF16)
        return 0

    lax.fori_loop(0, SEQ // chunk, out_body, 0)
    if need_ctx:
        hsum = bf_ref[0:CTX_LEN, :] + bb_ref[0:CTX_LEN, :]
        oc_ref[0] = (hsum * jax.nn.gelu(yc_ref[0].astype(F32))).astype(BF16)


def _rglru(proj_l, proj_c, conv_w, conv_b, wa, ba, wx, bx, lam, batch, need_ctx):
    def lat(sec):
        return pl.BlockSpec((1, SEQ, LANES), lambda b, k: (sec * NCB + k, b, 0))

    def ctx(sec):
        return pl.BlockSpec((1, CTX_LEN, LANES), lambda b, k: (sec * NCB + k, b, 0))

    def cols(rows):
        return pl.BlockSpec((rows, LANES), lambda b, k: (0, k))

    gate_w = pl.BlockSpec((2, 1, LANES, LANES), lambda b, k: (0, k, 0, 0))
    n_all = CTX_LEN + SEQ
    in_specs = [ctx(S_LX), lat(S_LX), lat(S_LY), cols(4), cols(1), gate_w, cols(2), gate_w, cols(2), cols(2)]
    args = [proj_c, proj_l, proj_l, conv_w, conv_b.reshape(1, D_MODEL), wa, ba, wx, bx, lam]
    out_specs = [pl.BlockSpec((1, SEQ, LANES), lambda b, k: (k, b, 0))]
    out_shape = [jax.ShapeDtypeStruct((LRU_BLOCKS, batch * SEQ, LANES), BF16)]
    if need_ctx:
        in_specs.append(ctx(S_LY))
        args.append(proj_c)
        out_specs.append(pl.BlockSpec((1, CTX_LEN, LANES), lambda b, k: (k, b, 0)))
        out_shape.append(jax.ShapeDtypeStruct((LRU_BLOCKS, batch * CTX_LEN, LANES), BF16))
    return pl.pallas_call(
        functools.partial(_lru_body, need_ctx=need_ctx),
        grid=(batch, LRU_BLOCKS),
        in_specs=in_specs,
        out_specs=out_specs,
        out_shape=out_shape,
        scratch_shapes=[pltpu.VMEM((CTX_LEN + 2 * SUBLANES, LANES), F32),
                        pltpu.VMEM((SEQ + 2 * SUBLANES, LANES), F32),
                        pltpu.VMEM((n_all, LANES), F32), pltpu.VMEM((n_all, LANES), F32),
                        pltpu.VMEM((n_all, LANES), F32), pltpu.VMEM((n_all, LANES), F32)],
        compiler_params=_params(("arbitrary", "arbitrary")),
        name="rglru",
    )(*args)


def _cat(ref):
    return jnp.concatenate([ref[g] for g in range(NCB)], axis=1)


def _merge_body(x_ref, mod_ref, g2_ref, a_ref, b_ref, c_ref, ga_ref, gb_ref, gc_ref, wb_ref, wo_ref,
                xo_ref, h2_ref):
    oa = jnp.dot(_cat(a_ref), wb_ref[0], preferred_element_type=F32)
    ob = jnp.dot(_cat(b_ref), wb_ref[1], preferred_element_type=F32)
    oc = jnp.dot(_cat(c_ref), wb_ref[2], preferred_element_type=F32)
    mixed = (jax.nn.sigmoid(_cat(ga_ref).astype(F32)) * oa + jax.nn.sigmoid(_cat(gb_ref).astype(F32)) * ob
             + jax.nn.sigmoid(_cat(gc_ref).astype(F32)) * oc)
    y = jnp.dot(mixed.astype(BF16), wo_ref[...], preferred_element_type=F32)
    xn = x_ref[...] + mod_ref[0, 2:3, :] * y
    xo_ref[...] = xn
    ms = jnp.mean(xn * xn, axis=-1, keepdims=True)
    hn = xn * lax.rsqrt(ms + NORM_EPS) * g2_ref[...]
    h2_ref[...] = hn * (1.0 + mod_ref[0, 4:5, :]) + mod_ref[0, 3:4, :]


def _merge(x, mod, mod_row, norm2_g, ret, na, lru, proj, wb, wo, tm):
    t = x.shape[0]

    def cb(sec):
        return pl.BlockSpec((NCB, tm, LANES), lambda i: (sec, i, 0))

    tok = pl.BlockSpec((tm, D_MODEL), lambda i: (i, 0))
    shape = jax.ShapeDtypeStruct((t, D_MODEL), F32)
    return pl.pallas_call(
        _merge_body,
        grid=(t // tm,),
        in_specs=[tok,
                  pl.BlockSpec((1, 6, D_MODEL), lambda i: (mod_row(i), 0, 0)),
                  pl.BlockSpec((1, D_MODEL), lambda i: (0, 0)),
                  cb(0), cb(0), cb(0), cb(S_GA), cb(S_GB), cb(S_GC),
                  pl.BlockSpec((3, D_MODEL, D_MODEL), lambda i: (0, 0, 0)),
                  pl.BlockSpec((D_MODEL, D_MODEL), lambda i: (0, 0))],
        out_specs=[tok, tok],
        out_shape=[shape, shape],
        compiler_params=_params(("arbitrary",)),
        name="merge",
    )(x, mod, norm2_g.reshape(1, D_MODEL), ret, na, lru, proj, proj, proj, wb, wo)


def _router_body(h_ref, wr_ref, br_ref, init_ref, ids_ref, w_ref, within_ref, cnt_ref, run_ref):
    tm = h_ref.shape[0]

    @pl.when(pl.program_id(0) == 0)
    def _():
        run_ref[...] = init_ref[...]

    logits = lax.dot_general(wr_ref[...], h_ref[...], (((1,), (1,)), ((), ())), preferred_element_type=F32,
                             precision=lax.Precision.HIGHEST) + br_ref[...]
    eidx = lax.broadcasted_iota(jnp.int32, (N_EXPERTS, tm), 0).astype(F32)
    cur = logits
    sel = jnp.zeros((N_EXPERTS, tm), F32)
    vals, hots = [], []
    for _ in range(TOP_K):
        m = jnp.max(cur, axis=0, keepdims=True)
        idx = jnp.min(jnp.where(cur == m, eidx, float(N_EXPERTS)), axis=0, keepdims=True)
        hot = eidx == idx
        vals.append(m)
        hots.append(hot)
        sel = sel + hot.astype(F32)
        cur = jnp.where(hot, -jnp.inf, cur)
    exps = [jnp.exp(v - vals[0]) for v in vals]
    den = functools.reduce(jnp.add, exps)

    ti = lax.broadcasted_iota(jnp.int32, (tm, tm), 0)
    tj = lax.broadcasted_iota(jnp.int32, (tm, tm), 1)
    upper = (ti < tj).astype(BF16)
    before = jnp.dot(sel.astype(BF16), upper, preferred_element_type=F32) + run_ref[:, 0:1]
    for k in range(TOP_K):
        ids_ref[k:k + 1, :] = jnp.sum(jnp.where(hots[k], eidx, 0.0), axis=0, keepdims=True).astype(jnp.int32)
        w_ref[k:k + 1, :] = exps[k] / den
        within_ref[k:k + 1, :] = jnp.sum(jnp.where(hots[k], before, 0.0), axis=0, keepdims=True).astype(jnp.int32)
    run_ref[...] = run_ref[...] + jnp.sum(sel, axis=1, keepdims=True)
    cnt_ref[...] = run_ref[...]


def _router(h2, w_router_t, b_router, init_counts, tm):
    t = h2.shape[0]
    tok4 = pl.BlockSpec((TOP_K, tm), lambda i: (0, i))
    cnt = pl.BlockSpec((N_EXPERTS, LANES), lambda i: (0, 0))
    return pl.pallas_call(
        _router_body,
        grid=(t // tm,),
        in_specs=[pl.BlockSpec((tm, D_MODEL), lambda i: (i, 0)),
                  pl.BlockSpec((N_EXPERTS, D_MODEL), lambda i: (0, 0)),
                  pl.BlockSpec((N_EXPERTS, 1), lambda i: (0, 0)),
                  cnt],
        out_specs=[tok4, tok4, tok4, cnt],
        out_shape=[jax.ShapeDtypeStruct((TOP_K, t), jnp.int32), jax.ShapeDtypeStruct((TOP_K, t), F32),
                   jax.ShapeDtypeStruct((TOP_K, t), jnp.int32),
                   jax.ShapeDtypeStruct((N_EXPERTS, LANES), F32)],
        scratch_shapes=[pltpu.VMEM((N_EXPERTS, LANES), F32)],
        compiler_params=_params(("arbitrary",)),
        name="router_topk",
    )(h2, w_router_t, b_router.reshape(N_EXPERTS, 1), init_counts)


def _row_copy(src_ref, src_row, dst_ref, dst_row, sem):
    return pltpu.make_async_copy(src_ref.at[pl.ds(pl.multiple_of(src_row * SUBLANES, SUBLANES), SUBLANES)],
                                 dst_ref.at[pl.ds(pl.multiple_of(dst_row * SUBLANES, SUBLANES), SUBLANES)], sem)


def _dispatch_body(pos_ref, h_ref, xg_in_ref, xg_ref, slab_ref, sem):
    del xg_in_ref
    tm = h_ref.shape[0]
    for s in range(NCB):
        slab_ref[pl.ds(s, tm, stride=SUBLANES), :] = h_ref[:, s * LANES:(s + 1) * LANES]

    def start(t, _):
        for k in range(TOP_K):
            _row_copy(slab_ref, t, xg_ref, pos_ref[k, t], sem).start(priority=k % 2)
        return 0

    lax.fori_loop(0, tm, start, 0)

    def wait(t, _):
        for k in range(TOP_K):
            _row_copy(slab_ref, t, xg_ref, pos_ref[k, t], sem).wait()
        return 0

    lax.fori_loop(0, tm, wait, 0)


def _dispatch(h2, pos, xg, tm):
    t = h2.shape[0]
    return pl.pallas_call(
        _dispatch_body,
        grid=(t // tm,),
        in_specs=[pl.BlockSpec((TOP_K, tm), lambda i: (0, i), memory_space=pltpu.SMEM),
                  pl.BlockSpec((tm, D_MODEL), lambda i: (i, 0)),
                  pl.BlockSpec(memory_space=pl.ANY)],
        out_specs=pl.BlockSpec(memory_space=pl.ANY),
        out_shape=jax.ShapeDtypeStruct(xg.shape, F32),
        scratch_shapes=[pltpu.VMEM((tm * SUBLANES, LANES), F32), pltpu.SemaphoreType.DMA],
        input_output_aliases={2: 0},
        compiler_params=_params(("arbitrary",)),
        name="moe_dispatch",
    )(pos, h2, xg)


DEINT_COLS = 2 * LANES
DEINT_ROWS = 512


def _deinterleave_matrix():
    src = np.arange(DEINT_COLS)[:, None]
    dst = np.arange(DEINT_COLS)[None, :]
    pick = np.where(dst < LANES, src == 2 * dst, src == 2 * (dst - LANES) + 1)
    return jnp.asarray(pick, BF16)


def _deint_body(w_ref, p_ref, g_ref, l_ref):
    for cidx in range(2 * D_MODEL // DEINT_COLS):
        chunk = w_ref[0, :, cidx * DEINT_COLS:(cidx + 1) * DEINT_COLS].astype(BF16)
        r = jnp.dot(chunk, p_ref[...], preferred_element_type=F32)
        g_ref[0, :, cidx * LANES:(cidx + 1) * LANES] = r[:, :LANES].astype(BF16)
        l_ref[0, :, cidx * LANES:(cidx + 1) * LANES] = r[:, LANES:].astype(BF16)


def _split_up_projection(w_up):
    out = pl.BlockSpec((1, DEINT_ROWS, D_MODEL), lambda e, i: (e, i, 0))
    shape = jax.ShapeDtypeStruct((N_EXPERTS, D_MODEL, D_MODEL), BF16)
    return pl.pallas_call(
        _deint_body,
        grid=(N_EXPERTS, D_MODEL // DEINT_ROWS),
        in_specs=[pl.BlockSpec((1, DEINT_ROWS, 2 * D_MODEL), lambda e, i: (e, i, 0)),
                  pl.BlockSpec((DEINT_COLS, DEINT_COLS), lambda e, i: (0, 0))],
        out_specs=[out, out],
        out_shape=[shape, shape],
        compiler_params=_params(("arbitrary", "arbitrary")),
        name="split_up_projection",
    )(w_up, _deinterleave_matrix())


def _expert_body(te_ref, nu_ref, x_ref, wg_ref, bg_ref, wl_ref, bl_ref, wd_ref, bd_ref, y_ref):
    del te_ref
    tm = x_ref.shape[0] // SUBLANES

    @pl.when(pl.program_id(0) < nu_ref[0])
    def _():
        x = jnp.concatenate([x_ref[pl.ds(s, tm, stride=SUBLANES), :] for s in range(NCB)], axis=1).astype(BF16)
        glu = jnp.minimum(jnp.dot(x, wg_ref[0], preferred_element_type=F32) + bg_ref[0], SWIGLU_LIMIT)
        lin = jnp.clip(jnp.dot(x, wl_ref[0], preferred_element_type=F32) + bl_ref[0], -SWIGLU_LIMIT, SWIGLU_LIMIT)
        act = glu * jax.nn.sigmoid(SWIGLU_ALPHA * glu) * (lin + 1.0)
        y = jnp.dot(act.astype(BF16), wd_ref[0], preferred_element_type=F32) + bd_ref[0]
        for s in range(NCB):
            y_ref[pl.ds(s, tm, stride=SUBLANES), :] = y[:, s * LANES:(s + 1) * LANES]

    @pl.when(pl.program_id(0) >= nu_ref[0])
    def _():
        y_ref[...] = jnp.zeros_like(y_ref)


def _experts(xg, tile_expert, n_used, wg, bg, wl, bl, wd, bd):
    tm = EXPERT_TILE
    n_tiles = xg.shape[0] // (tm * SUBLANES)

    def rows(m, te, nu):
        return (jnp.minimum(m, nu[0] - 1), 0)

    def per_expert(shape):
        return pl.BlockSpec((1,) + shape, lambda m, te, nu: (te[m], 0, 0))

    grid_spec = pltpu.PrefetchScalarGridSpec(
        num_scalar_prefetch=2,
        grid=(n_tiles,),
        in_specs=[pl.BlockSpec((tm * SUBLANES, LANES), rows),
                  per_expert((D_MODEL, D_MODEL)), per_expert((1, D_MODEL)),
                  per_expert((D_MODEL, D_MODEL)), per_expert((1, D_MODEL)),
                  per_expert((D_MODEL, D_MODEL)), per_expert((1, D_MODEL))],
        out_specs=pl.BlockSpec((tm * SUBLANES, LANES), lambda m, te, nu: (m, 0)),
    )
    return pl.pallas_call(
        _expert_body,
        grid_spec=grid_spec,
        out_shape=jax.ShapeDtypeStruct(xg.shape, F32),
        compiler_params=_params(("arbitrary",)),
        name="moe_experts",
    )(tile_expert, n_used, xg, wg, bg, wl, bl, wd, bd)


def _combine_body(pos_ref, x_ref, w_ref, mod_ref, fg_ref, yg_ref, o_ref, ybuf_ref, sem, *, final_norm):
    tm = x_ref.shape[0]

    def start(t, _):
        for k in range(TOP_K):
            _row_copy(yg_ref, pos_ref[k, t], ybuf_ref.at[k], t, sem).start(priority=k % 2)
        return 0

    lax.fori_loop(0, tm, start, 0)

    def wait(t, _):
        for k in range(TOP_K):
            _row_copy(yg_ref, pos_ref[k, t], ybuf_ref.at[k], t, sem).wait()
        return 0

    lax.fori_loop(0, tm, wait, 0)

    acc = jnp.zeros((tm, D_MODEL), F32)
    for k in range(TOP_K):
        yk = jnp.concatenate([ybuf_ref[k, pl.ds(s, tm, stride=SUBLANES), :] for s in range(NCB)], axis=1)
        acc = acc + w_ref[:, k:k + 1] * yk
    xn = x_ref[...] + mod_ref[0, 5:6, :] * acc
    if final_norm:
        ms = jnp.mean(xn * xn, axis=-1, keepdims=True)
        xn = xn * lax.rsqrt(ms + NORM_EPS) * fg_ref[...]
    o_ref[...] = xn


def _combine(x, w_tok, pos, yg, mod, mod_row, final_g, tm, final_norm):
    t = x.shape[0]
    return pl.pallas_call(
        functools.partial(_combine_body, final_norm=final_norm),
        grid=(t // tm,),
        in_specs=[pl.BlockSpec((TOP_K, tm), lambda i: (0, i), memory_space=pltpu.SMEM),
                  pl.BlockSpec((tm, D_MODEL), lambda i: (i, 0)),
                  pl.BlockSpec((tm, TOP_K), lambda i: (i, 0)),
                  pl.BlockSpec((1, 6, D_MODEL), lambda i: (mod_row(i), 0, 0)),
                  pl.BlockSpec((1, D_MODEL), lambda i: (0, 0)),
                  pl.BlockSpec(memory_space=pl.ANY)],
        out_specs=pl.BlockSpec((tm, D_MODEL), lambda i: (i, 0)),
        out_shape=jax.ShapeDtypeStruct((t, D_MODEL), F32),
        scratch_shapes=[pltpu.VMEM((TOP_K, tm * SUBLANES, LANES), F32), pltpu.SemaphoreType.DMA],
        compiler_params=_params(("arbitrary",)),
        name="moe_combine",
    )(pos, x, w_tok, mod, final_g.reshape(1, D_MODEL), yg)


def _moe(streams, mod, w_router, b_router, w_up, b_up, w_down, b_down, final_g, final_norm):
    te = EXPERT_TILE
    w_router_t = w_router.T
    counts = jnp.zeros((N_EXPERTS, LANES), F32)
    routed = []
    for _, h2, _ in streams:
        ids, w, within, counts = _router(h2, w_router_t, b_router, counts, min(512, h2.shape[0]))
        routed.append((ids, w, within))
    total = sum(h2.shape[0] for _, h2, _ in streams)
    cnt = counts[:, 0].astype(jnp.int32)
    padded = ((cnt + te - 1) // te) * te
    ends = jnp.cumsum(padded)
    starts = ends - padded
    n_tiles = total * TOP_K // te + N_EXPERTS
    tile_ids = jnp.arange(n_tiles, dtype=jnp.int32)
    tile_expert = jnp.minimum(jnp.sum((ends // te)[None, :] <= tile_ids[:, None], axis=1),
                              N_EXPERTS - 1).astype(jnp.int32)
    n_used = (ends[-1:] // te).astype(jnp.int32)
    expert_ids = jnp.arange(N_EXPERTS, dtype=jnp.int32)[:, None, None]

    wg, wl = _split_up_projection(w_up)
    bg = b_up[:, None, 0::2]
    bl = b_up[:, None, 1::2]

    xg = jnp.zeros((n_tiles * te * SUBLANES, LANES), F32)
    positions = []
    for (_, h2, _), (ids, _, within) in zip(streams, routed):
        pos = within + jnp.sum(jnp.where(ids[None] == expert_ids, starts[:, None, None], 0), axis=0)
        positions.append(pos)
        xg = _dispatch(h2, pos, xg, ROW_DMA_TILE)
    yg = _experts(xg, tile_expert, n_used, wg, bg, wl, bl, w_down.astype(BF16), b_down[:, None, :])
    return [_combine(x, w.T, pos, yg, mod, mod_row, final_g, ROW_DMA_TILE, final_norm)
            for (x, _, mod_row), (_, w, _), pos in zip(streams, routed, positions)]


def _rope_tables():
    tok = jnp.arange(SEQ)
    nf = 32
    inv = ROPE_BASE ** (-jnp.arange(nf, dtype=F32) / nf)
    ang_r = (tok // GRID_W).astype(F32)[:, None] * inv
    ang_c = (tok % GRID_W).astype(F32)[:, None] * inv
    cos = jnp.concatenate([jnp.cos(ang_r)] * 2 + [jnp.cos(ang_c)] * 2, axis=-1)
    sin = jnp.concatenate([-jnp.sin(ang_r), jnp.sin(ang_r), -jnp.sin(ang_c), jnp.sin(ang_c)], axis=-1)
    return cos, sin


def kernel(x, c, ctx, c_ctx, w_mod, b_mod, norm1_g, norm2_g, w_mix_in, ret_decay_fwd, ret_decay_bwd, na_rel_bias, lru_conv_w, lru_conv_b, lru_gate_a_w, lru_gate_a_b, lru_gate_x_w, lru_gate_x_b, lru_lambda, w_branch, w_mix_out, w_router, b_router, w_expert_up, b_expert_up, w_expert_down, b_expert_down, final_norm_g):
    batch = x.shape[0]
    depth = w_mod.shape[0]
    assert x.shape[1:] == (SEQ, D_MODEL) and ctx.shape[1:] == (CTX_LEN, D_MODEL)
    tm_l = TOKEN_TILE
    tm_c = min(TOKEN_TILE, batch * CTX_LEN)
    tm_ml = MERGE_TILE
    tm_mc = min(MERGE_TILE, batch * CTX_LEN)
    assert SEQ % tm_l == 0 and (batch * CTX_LEN) % tm_c == 0 and (batch * CTX_LEN) % tm_mc == 0

    def lat_row(tile_rows):
        return lambda i: i // (SEQ // tile_rows)

    def ctx_row(tile_rows):
        del tile_rows
        return lambda i: batch

    xl = x.reshape(batch * SEQ, D_MODEL)
    xc = ctx.reshape(batch * CTX_LEN, D_MODEL)
    mp = -(-(batch + 1) // SUBLANES) * SUBLANES
    cc = jnp.zeros((mp, D_MODEL), F32).at[:batch].set(c).at[batch].set(c_ctx)
    cos, sin_signed = _rope_tables()

    for l in range(depth):
        last = l == depth - 1
        need_ctx = not last
        mod = _modulation(cc, w_mod[l], b_mod[l])
        w_in = w_mix_in[l].astype(BF16)
        proj_l = _input_projection(xl, mod, lat_row(tm_l), norm1_g[l], w_in, tm_l)
        proj_c = _input_projection(xc, mod, ctx_row(tm_c), norm1_g[l], w_in, tm_c)
        lg = jnp.stack([jax.nn.log_sigmoid(ret_decay_fwd[l].astype(F32)),
                        jax.nn.log_sigmoid(ret_decay_bwd[l].astype(F32))])
        ret = _retention(proj_l, proj_c, lg, cos, sin_signed, batch, need_ctx)
        na = _neighbourhood(proj_l, proj_c, _na_bias_table(na_rel_bias[l]), batch, need_ctx)
        lru = _rglru(proj_l, proj_c, lru_conv_w[l], lru_conv_b[l], lru_gate_a_w[l].astype(BF16), lru_gate_a_b[l],
                     lru_gate_x_w[l].astype(BF16), lru_gate_x_b[l], lru_lambda[l], batch, need_ctx)
        wb = w_branch[l].astype(BF16)
        wo = w_mix_out[l].astype(BF16)
        xl, h2l = _merge(xl, mod, lat_row(tm_ml), norm2_g[l], ret[0], na[0], lru[0], proj_l, wb, wo, tm_ml)
        streams = [(xl, h2l, lat_row(ROW_DMA_TILE))]
        if need_ctx:
            xc, h2c = _merge(xc, mod, ctx_row(tm_mc), norm2_g[l], ret[1], na[1], lru[1], proj_c, wb, wo, tm_mc)
            streams = [(xc, h2c, ctx_row(ROW_DMA_TILE))] + streams
        outs = _moe(streams, mod, w_router[l], b_router[l], w_expert_up[l], b_expert_up[l], w_expert_down[l],
                    b_expert_down[l], final_norm_g, last)
        xl = outs[-1]
        if need_ctx:
            xc = outs[0]
    return xl.reshape(batch, SEQ, D_MODEL)
```

```python
import functools

import numpy as np
import jax
import jax.numpy as jnp
from jax import lax
from jax.experimental import pallas as pl
from jax.experimental.pallas import tpu as pltpu

F32 = jnp.float32
BF16 = jnp.bfloat16

D_MODEL = 1024
SEQ = 2048
CTX_LEN = 256
GRID_W = 64
N_ROWS = SEQ // GRID_W
LANES = 128
SUBLANES = 8
NCB = D_MODEL // LANES
N_SEC = 12
RET_HEADS = 8
RET_CHUNK = 256
NA_HEADS = 16
NA_KH = 8
NA_KW = 16
NA_GROUP = 4
NA_GROUPS_PER_STEP = 2
NA_SLAB = 12
LRU_SCAN_TILES = 4
LRU_BLOCKS = 8
LRU_C = 8.0
N_EXPERTS = 32
TOP_K = 4
SWIGLU_LIMIT = 7.0
SWIGLU_ALPHA = 1.702
ROPE_BASE = 10000.0
NORM_EPS = 1e-6
NEG_BIG = -1e30
TOKEN_TILE = 1024
MERGE_TILE = 512
ROW_DMA_TILE = 256
EXPERT_TILE = 256
VMEM_LIMIT = 56 * 1024 * 1024

S_RQ, S_RK, S_RV, S_RG, S_NQ, S_NK, S_NV, S_LX, S_LY, S_GA, S_GB, S_GC = range(12)


def _nt_dot(a, b):
    return lax.dot_general(a, b, (((1,), (1,)), ((), ())), preferred_element_type=F32)


def _tn_dot(a, b):
    return lax.dot_general(a, b, (((0,), (0,)), ((), ())), preferred_element_type=F32)


def _params(sem, vmem=VMEM_LIMIT):
    return pltpu.CompilerParams(dimension_semantics=sem, vmem_limit_bytes=vmem)


def _mod_body(c_ref, w_ref, b_ref, o_ref):
    c = c_ref[...]
    s = c * jax.nn.sigmoid(c)
    o_ref[...] = jnp.dot(s, w_ref[...], preferred_element_type=F32,
                         precision=lax.Precision.HIGHEST) + b_ref[...]


def _modulation(cc, w_mod, b_mod):
    mp = cc.shape[0]
    out = pl.pallas_call(
        _mod_body,
        grid=(6,),
        in_specs=[pl.BlockSpec((mp, D_MODEL), lambda j: (0, 0)),
                  pl.BlockSpec((D_MODEL, D_MODEL), lambda j: (0, j)),
                  pl.BlockSpec((1, D_MODEL), lambda j: (0, j))],
        out_specs=pl.BlockSpec((mp, D_MODEL), lambda j: (0, j)),
        out_shape=jax.ShapeDtypeStruct((mp, 6 * D_MODEL), F32),
        compiler_params=_params(("arbitrary",)),
        name="adaln_mod",
    )(cc, w_mod, b_mod.reshape(1, 6 * D_MODEL))
    return out.reshape(mp, 6, D_MODEL)


def _inproj_body(x_ref, mod_ref, g_ref, w_ref, o_ref, h_ref):
    @pl.when(pl.program_id(1) == 0)
    def _():
        x = x_ref[...]
        ms = jnp.mean(x * x, axis=-1, keepdims=True)
        y = x * lax.rsqrt(ms + NORM_EPS) * g_ref[...]
        h_ref[...] = (y * (1.0 + mod_ref[0, 1:2, :]) + mod_ref[0, 0:1, :]).astype(BF16)

    acc = jnp.dot(h_ref[...], w_ref[...], preferred_element_type=F32)
    for g in range(NCB):
        o_ref[g] = acc[:, g * LANES:(g + 1) * LANES].astype(BF16)


def _input_projection(x, mod, mod_row, norm_g, w_in, tm):
    t = x.shape[0]
    return pl.pallas_call(
        _inproj_body,
        grid=(t // tm, N_SEC),
        in_specs=[pl.BlockSpec((tm, D_MODEL), lambda i, j: (i, 0)),
                  pl.BlockSpec((1, 6, D_MODEL), lambda i, j: (mod_row(i), 0, 0)),
                  pl.BlockSpec((1, D_MODEL), lambda i, j: (0, 0)),
                  pl.BlockSpec((D_MODEL, D_MODEL), lambda i, j: (0, j))],
        out_specs=pl.BlockSpec((NCB, tm, LANES), lambda i, j: (j, i, 0)),
        out_shape=jax.ShapeDtypeStruct((N_SEC * NCB, t, LANES), BF16),
        scratch_shapes=[pltpu.VMEM((tm, D_MODEL), BF16)],
        compiler_params=_params(("arbitrary", "arbitrary")),
        name="in_proj",
    )(x, mod, norm_g.reshape(1, D_MODEL), w_in)


def _rope(x, cos, sin_signed, first_half):
    partner = jnp.where(first_half, pltpu.roll(x, LANES - 32, 1), pltpu.roll(x, 32, 1))
    return x * cos + partner * sin_signed


def _head_rms_gate(o, gate):
    o = o * lax.rsqrt(jnp.mean(o * o, axis=-1, keepdims=True) + NORM_EPS)
    return o * (gate * jax.nn.sigmoid(gate))


def _ret_body(*refs, need_ctx):
    if need_ctx:
        (lg_ref, q_ref, k_ref, v_ref, rg_ref, kc_ref, vc_ref, cos_ref, sin_ref, qc_ref, rgc_ref,
         o_ref, oc_ref, kr_ref, sb_ref, dc_ref) = refs
    else:
        (lg_ref, q_ref, k_ref, v_ref, rg_ref, kc_ref, vc_ref, cos_ref, sin_ref,
         o_ref, kr_ref, sb_ref, dc_ref) = refs
    h = pl.program_id(1)
    c = RET_CHUNK
    n_chunks = SEQ // c
    lgf = lg_ref[0, h]
    lgb = lg_ref[1, h]
    ks = float(LANES) ** -0.5

    ia = lax.broadcasted_iota(jnp.int32, (c, 1), 0).astype(F32)
    ii = lax.broadcasted_iota(jnp.int32, (c, c), 0)
    jj = lax.broadcasted_iota(jnp.int32, (c, c), 1)
    diff = (ii - jj).astype(F32)
    dc_ref[...] = jnp.where(diff >= 0, jnp.exp(diff * lgf), jnp.exp(-diff * lgb))
    w_f = jnp.exp((c - 1.0 - ia) * lgf)
    w_b = jnp.exp(ia * lgb)
    x_f = jnp.exp((ia + 1.0) * lgf)
    x_b = jnp.exp((c - ia) * lgb)
    g_f = jnp.exp(c * lgf)
    g_b = jnp.exp(c * lgb)
    lane = lax.broadcasted_iota(jnp.int32, (1, LANES), 1)
    first_half = (lane % 64) < 32

    kc = kc_ref[0].astype(F32)
    vc = vc_ref[0]
    sf0 = _tn_dot((kc * w_f).astype(BF16), vc)
    sb0 = _tn_dot((kc * w_b).astype(BF16), vc)

    def bwd_body(i, sb):
        n = n_chunks - 1 - i
        r0 = pl.multiple_of(n * c, c)
        sb_ref[n] = sb
        k = _rope(k_ref[0, pl.ds(r0, c), :].astype(F32), cos_ref[pl.ds(r0, c), :], sin_ref[pl.ds(r0, c), :],
                  first_half)
        kr_ref[pl.ds(r0, c), :] = k.astype(BF16)
        ub = _tn_dot((k * w_b).astype(BF16), v_ref[0, pl.ds(r0, c), :])
        return ub + g_b * sb

    lax.fori_loop(0, n_chunks, bwd_body, sb0, unroll=True)

    def fwd_body(n, sf):
        r0 = pl.multiple_of(n * c, c)
        q = _rope(q_ref[0, pl.ds(r0, c), :].astype(F32), cos_ref[pl.ds(r0, c), :], sin_ref[pl.ds(r0, c), :],
                  first_half)
        k = kr_ref[pl.ds(r0, c), :]
        v = v_ref[0, pl.ds(r0, c), :]
        s = _nt_dot(q.astype(BF16), k)
        o = jnp.dot((s * dc_ref[...]).astype(BF16), v, preferred_element_type=F32)
        o = o + jnp.dot((q * x_f).astype(BF16), sf.astype(BF16), preferred_element_type=F32)
        o = o + jnp.dot((q * x_b).astype(BF16), sb_ref[n].astype(BF16), preferred_element_type=F32)
        o = _head_rms_gate(o * ks, rg_ref[0, pl.ds(r0, c), :].astype(F32))
        o_ref[0, pl.ds(r0, c), :] = o.astype(BF16)
        uf = _tn_dot((k.astype(F32) * w_f).astype(BF16), v)
        return uf + g_f * sf

    lax.fori_loop(0, n_chunks, fwd_body, sf0, unroll=True)

    if need_ctx:
        s = _nt_dot(qc_ref[0], kc_ref[0])
        o = jnp.dot((s * dc_ref[...]).astype(BF16), vc, preferred_element_type=F32)
        oc_ref[0] = _head_rms_gate(o * ks, rgc_ref[0].astype(F32)).astype(BF16)


def _retention(proj_l, proj_c, lg, cos, sin_signed, batch, need_ctx):
    def lat(sec):
        return pl.BlockSpec((1, SEQ, LANES), lambda b, h: (sec * NCB + h, b, 0))

    def ctx(sec):
        return pl.BlockSpec((1, CTX_LEN, LANES), lambda b, h: (sec * NCB + h, b, 0))

    table = pl.BlockSpec((SEQ, LANES), lambda b, h: (0, 0))
    in_specs = [pl.BlockSpec(memory_space=pltpu.SMEM),
                lat(S_RQ), lat(S_RK), lat(S_RV), lat(S_RG), ctx(S_RK), ctx(S_RV), table, table]
    args = [lg, proj_l, proj_l, proj_l, proj_l, proj_c, proj_c, cos, sin_signed]
    out_specs = [pl.BlockSpec((1, SEQ, LANES), lambda b, h: (h, b, 0))]
    out_shape = [jax.ShapeDtypeStruct((RET_HEADS, batch * SEQ, LANES), BF16)]
    if need_ctx:
        in_specs += [ctx(S_RQ), ctx(S_RG)]
        args += [proj_c, proj_c]
        out_specs.append(pl.BlockSpec((1, CTX_LEN, LANES), lambda b, h: (h, b, 0)))
        out_shape.append(jax.ShapeDtypeStruct((RET_HEADS, batch * CTX_LEN, LANES), BF16))
    return pl.pallas_call(
        functools.partial(_ret_body, need_ctx=need_ctx),
        grid=(batch, RET_HEADS),
        in_specs=in_specs,
        out_specs=out_specs,
        out_shape=out_shape,
        scratch_shapes=[pltpu.VMEM((SEQ, LANES), BF16),
                        pltpu.VMEM((SEQ // RET_CHUNK, LANES, LANES), F32),
                        pltpu.VMEM((RET_CHUNK, RET_CHUNK), F32)],
        compiler_params=_params(("arbitrary", "arbitrary")),
        name="retention",
    )(*args)


def _na_softmax_pv(s_parts, v_parts):
    m = functools.reduce(jnp.maximum, [jnp.max(s, axis=-1, keepdims=True) for s in s_parts])
    p_parts = [jnp.exp(s - m) for s in s_parts]
    den = functools.reduce(jnp.add, [jnp.sum(p, axis=-1, keepdims=True) for p in p_parts])
    o = functools.reduce(jnp.add, [jnp.dot(p.astype(BF16), v, preferred_element_type=F32)
                                   for p, v in zip(p_parts, v_parts)])
    return o / den


def _split_heads(q, low):
    zero = jnp.zeros_like(q)
    return jnp.concatenate([jnp.where(low, q, zero), jnp.where(low, zero, q)], axis=0)


def _na_body(*refs, need_ctx):
    if need_ctx:
        q_ref, k_ref, v_ref, kc_ref, vc_ref, bias_ref, qc_ref, o_ref, oc_ref = refs
    else:
        q_ref, k_ref, v_ref, kc_ref, vc_ref, bias_ref, o_ref = refs
    lane = lax.broadcasted_iota(jnp.int32, (1, LANES), 1)
    low = lane < 64
    scale = 64.0 ** -0.5
    kc = kc_ref[0]
    vc = vc_ref[0]
    n_q = NA_GROUP * GRID_W
    n_groups = N_ROWS // NA_GROUP

    def group_scores(gi):
        r0 = jnp.clip(gi * NA_GROUP - NA_KH // 2, 0, N_ROWS - NA_SLAB)
        case = jnp.where(gi == 0, 0, jnp.where(gi == n_groups - 1, 2, 1))
        q0 = pl.multiple_of(gi * n_q, n_q)
        k0 = pl.multiple_of(r0 * GRID_W, GRID_W)
        q = q_ref[0, pl.ds(q0, n_q), :] * scale
        kw = k_ref[0, pl.ds(k0, NA_SLAB * GRID_W), :]
        vw = v_ref[0, pl.ds(k0, NA_SLAB * GRID_W), :]
        zero = jnp.zeros_like(q)
        units = []
        for hd in range(2):
            qh = jnp.where(low, q, zero) if hd == 0 else jnp.where(low, zero, q)
            units.append(([_nt_dot(qh, kw) + bias_ref[hd, case], _nt_dot(qh, kc)], [vw, vc]))
        return q0, units

    def group_body(i, carry):
        staged = [group_scores(i * NA_GROUPS_PER_STEP + j) for j in range(NA_GROUPS_PER_STEP)]
        for q0, units in staged:
            o0, o1 = [_na_softmax_pv(s_parts, v_parts) for s_parts, v_parts in units]
            o_ref[0, pl.ds(q0, n_q), :] = jnp.where(low, o0, o1).astype(BF16)
        return carry

    lax.fori_loop(0, n_groups // NA_GROUPS_PER_STEP, group_body, 0)

    if need_ctx:
        q2 = _split_heads(qc_ref[0] * scale, low)
        o = _na_softmax_pv([_nt_dot(q2, kc)], [vc])
        oc_ref[0] = jnp.where(low, o[:CTX_LEN], o[CTX_LEN:]).astype(BF16)


def _na_bias_table(rpb):
    w = np.arange(GRID_W)
    c0 = np.clip(w - NA_KW // 2, 0, GRID_W - NA_KW)
    kcol = np.arange(GRID_W)
    in_win = (kcol[None, :] >= c0[:, None]) & (kcol[None, :] < c0[:, None] + NA_KW)
    dc = kcol[None, :] - w[:, None] + (NA_KW - 1)
    pick = ((dc[None] == np.arange(2 * NA_KW - 1)[:, None, None]) & in_win[None]).astype(np.float32)
    band = jnp.einsum("hrc,cwk->hrwk", rpb.astype(F32), pick, precision=lax.Precision.HIGHEST)
    band = jnp.where(in_win[None, None], band, NEG_BIG)

    def rows_of_group(gi):
        r = gi * NA_GROUP + np.arange(NA_GROUP)[:, None]
        krow = np.clip(gi * NA_GROUP - NA_KH // 2, 0, N_ROWS - NA_SLAB) + np.arange(NA_SLAB)[None, :]
        r0 = np.clip(r - NA_KH // 2, 0, N_ROWS - NA_KH)
        return krow - r + (NA_KH - 1), (krow >= r0) & (krow < r0 + NA_KH)

    n_groups = N_ROWS // NA_GROUP
    cases = [rows_of_group(0), rows_of_group(1), rows_of_group(n_groups - 1)]
    for gi in range(1, n_groups - 1):
        dr_i, ok_i = rows_of_group(gi)
        assert (ok_i == cases[1][1]).all() and (dr_i[ok_i] == cases[1][0][cases[1][1]]).all()
    dr = np.stack([np.where(ok, d, 0) for d, ok in cases])
    ok = np.stack([ok for _, ok in cases])
    tab = jnp.take(band, jnp.asarray(dr.reshape(-1), jnp.int32), axis=1)
    tab = tab.reshape(NA_HEADS, 3, NA_GROUP, NA_SLAB, GRID_W, GRID_W)
    tab = jnp.where(ok[None, :, :, :, None, None], tab, NEG_BIG)
    return tab.transpose(0, 1, 2, 4, 3, 5).reshape(NA_HEADS, 3, NA_GROUP * GRID_W, NA_SLAB * GRID_W)


def _neighbourhood(proj_l, proj_c, bias, batch, need_ctx):
    n_pairs = NA_HEADS // 2

    def lat(sec):
        return pl.BlockSpec((1, SEQ, LANES), lambda p, b: (sec * NCB + p, b, 0))

    def ctx(sec):
        return pl.BlockSpec((1, CTX_LEN, LANES), lambda p, b: (sec * NCB + p, b, 0))

    in_specs = [lat(S_NQ), lat(S_NK), lat(S_NV), ctx(S_NK), ctx(S_NV),
                pl.BlockSpec((2, 3, NA_GROUP * GRID_W, NA_SLAB * GRID_W), lambda p, b: (p, 0, 0, 0))]
    args = [proj_l, proj_l, proj_l, proj_c, proj_c, bias]
    out_specs = [pl.BlockSpec((1, SEQ, LANES), lambda p, b: (p, b, 0))]
    out_shape = [jax.ShapeDtypeStruct((n_pairs, batch * SEQ, LANES), BF16)]
    if need_ctx:
        in_specs.append(ctx(S_NQ))
        args.append(proj_c)
        out_specs.append(pl.BlockSpec((1, CTX_LEN, LANES), lambda p, b: (p, b, 0)))
        out_shape.append(jax.ShapeDtypeStruct((n_pairs, batch * CTX_LEN, LANES), BF16))
    return pl.pallas_call(
        functools.partial(_na_body, need_ctx=need_ctx),
        grid=(n_pairs, batch),
        in_specs=in_specs,
        out_specs=out_specs,
        out_shape=out_shape,
        compiler_params=_params(("arbitrary", "arbitrary")),
        name="neighbourhood_attention",
    )(*args)


def _scan_tile(a, b, row, reverse):
    for sh in (1, 2, 4):
        if reverse:
            a_s = pltpu.roll(a, SUBLANES - sh, 0)
            b_s = pltpu.roll(b, SUBLANES - sh, 0)
            m = row < SUBLANES - sh
        else:
            a_s = pltpu.roll(a, sh, 0)
            b_s = pltpu.roll(b, sh, 0)
            m = row >= sh
        b = jnp.where(m, a * b_s + b, b)
        a = jnp.where(m, a * a_s, a)
    return a, b


def _lru_body(*refs, need_ctx):
    if need_ctx:
        (xc_ref, xl_ref, yl_ref, cw_ref, cb_ref, wa_ref, ba_ref, wx_ref, bx_ref, lam_ref, yc_ref,
         o_ref, oc_ref, xpc_ref, xpl_ref, af_ref, bf_ref, ab_ref, bb_ref) = refs
    else:
        (xc_ref, xl_ref, yl_ref, cw_ref, cb_ref, wa_ref, ba_ref, wx_ref, bx_ref, lam_ref,
         o_ref, xpc_ref, xpl_ref, af_ref, bf_ref, ab_ref, bb_ref) = refs
    pad = SUBLANES
    zpad = jnp.zeros((pad, LANES), F32)
    xpc_ref[0:pad, :] = zpad
    xpc_ref[pad:pad + CTX_LEN, :] = xc_ref[0].astype(F32)
    xpc_ref[pad + CTX_LEN:pad + CTX_LEN + pad, :] = zpad
    xpl_ref[0:pad, :] = zpad
    xpl_ref[pad:pad + SEQ, :] = xl_ref[0].astype(F32)
    xpl_ref[pad + SEQ:pad + SEQ + pad, :] = zpad

    cw = cw_ref[...]
    cb = cb_ref[...]
    a_refs = (af_ref, ab_ref)
    b_refs = (bf_ref, bb_ref)
    chunk = CTX_LEN

    def pointwise(xp_ref, t0, dst0):
        taps = [xp_ref[pad + t0 - 1 + j:pad + t0 - 1 + j + chunk, :] for j in range(4)]
        u = cw[0:1] * taps[0] + cw[1:2] * taps[1] + cw[2:3] * taps[2] + cw[3:4] * taps[3] + cb
        ub = u.astype(BF16)
        for d in range(2):
            r = jax.nn.sigmoid(jnp.dot(ub, wa_ref[d, 0], preferred_element_type=F32) + ba_ref[d:d + 1, :])
            i = jax.nn.sigmoid(jnp.dot(ub, wx_ref[d, 0], preferred_element_type=F32) + bx_ref[d:d + 1, :])
            log_a = (-LRU_C) * r * jax.nn.softplus(-lam_ref[d:d + 1, :])
            a = jnp.exp(log_a)
            a_refs[d][dst0:dst0 + chunk, :] = a
            b_refs[d][dst0:dst0 + chunk, :] = jnp.sqrt((1.0 - a) * (1.0 + a)) * (i * u)

    pointwise(xpc_ref, 0, 0)
    for cidx in range(SEQ // chunk):
        pointwise(xpl_ref, cidx * chunk, CTX_LEN + cidx * chunk)

    row = lax.broadcasted_iota(jnp.int32, (SUBLANES, LANES), 0)
    n_ctx_tiles = CTX_LEN // SUBLANES
    n_all_tiles = (CTX_LEN + SEQ) // SUBLANES

    def scan_group(jf, jb, carry):
        hf, hb = carry
        rows_f = [pl.multiple_of((jf + g) * SUBLANES, SUBLANES) for g in range(LRU_SCAN_TILES)]
        rows_b = [pl.multiple_of((jb - g) * SUBLANES, SUBLANES) for g in range(LRU_SCAN_TILES)]
        scans_f = [_scan_tile(af_ref[pl.ds(r, SUBLANES), :], bf_ref[pl.ds(r, SUBLANES), :], row, False)
                   for r in rows_f]
        scans_b = [_scan_tile(ab_ref[pl.ds(r, SUBLANES), :], bb_ref[pl.ds(r, SUBLANES), :], row, True)
                   for r in rows_b]
        out_f, out_b = [], []
        for (af, bf), (ab, bb) in zip(scans_f, scans_b):
            tf = bf + af * hf
            tb = bb + ab * hb
            out_f.append(tf)
            out_b.append(tb)
            hf, hb = tf[SUBLANES - 1:SUBLANES, :], tb[0:1, :]
        for r, t in zip(rows_f, out_f):
            bf_ref[pl.ds(r, SUBLANES), :] = t
        for r, t in zip(rows_b, out_b):
            bb_ref[pl.ds(r, SUBLANES), :] = t
        return hf, hb

    g = LRU_SCAN_TILES
    h0 = jnp.zeros((1, LANES), F32)
    carry = lax.fori_loop(0, n_ctx_tiles // g,
                          lambda i, cr: scan_group(i * g, n_ctx_tiles - 1 - i * g, cr), (h0, h0))
    lax.fori_loop(0, (n_all_tiles - n_ctx_tiles) // g,
                  lambda i, cr: scan_group(n_ctx_tiles + i * g, n_all_tiles - 1 - i * g, cr), carry)

    def out_body(cidx, _):
        r0 = pl.multiple_of(cidx * chunk, chunk)
        hsum = bf_ref[pl.ds(CTX_LEN + r0, chunk), :] + bb_ref[pl.ds(CTX_LEN + r0, chunk), :]
        o_ref[0, pl.ds(r0, chunk), :] = (hsum * jax.nn.gelu(yl_ref[0, pl.ds(r0, chunk), :].astype(F32))).astype(BF16)
        return 0

    lax.fori_loop(0, SEQ // chunk, out_body, 0)
    if need_ctx:
        hsum = bf_ref[0:CTX_LEN, :] + bb_ref[0:CTX_LEN, :]
        oc_ref[0] = (hsum * jax.nn.gelu(yc_ref[0].astype(F32))).astype(BF16)


def _rglru(proj_l, proj_c, conv_w, conv_b, wa, ba, wx, bx, lam, batch, need_ctx):
    def lat(sec):
        return pl.BlockSpec((1, SEQ, LANES), lambda b, k: (sec * NCB + k, b, 0))

    def ctx(sec):
        return pl.BlockSpec((1, CTX_LEN, LANES), lambda b, k: (sec * NCB + k, b, 0))

    def cols(rows):
        return pl.BlockSpec((rows, LANES), lambda b, k: (0, k))

    gate_w = pl.BlockSpec((2, 1, LANES, LANES), lambda b, k: (0, k, 0, 0))
    n_all = CTX_LEN + SEQ
    in_specs = [ctx(S_LX), lat(S_LX), lat(S_LY), cols(4), cols(1), gate_w, cols(2), gate_w, cols(2), cols(2)]
    args = [proj_c, proj_l, proj_l, conv_w, conv_b.reshape(1, D_MODEL), wa, ba, wx, bx, lam]
    out_specs = [pl.BlockSpec((1, SEQ, LANES), lambda b, k: (k, b, 0))]
    out_shape = [jax.ShapeDtypeStruct((LRU_BLOCKS, batch * SEQ, LANES), BF16)]
    if need_ctx:
        in_specs.append(ctx(S_LY))
        args.append(proj_c)
        out_specs.append(pl.BlockSpec((1, CTX_LEN, LANES), lambda b, k: (k, b, 0)))
        out_shape.append(jax.ShapeDtypeStruct((LRU_BLOCKS, batch * CTX_LEN, LANES), BF16))
    return pl.pallas_call(
        functools.partial(_lru_body, need_ctx=need_ctx),
        grid=(batch, LRU_BLOCKS),
        in_specs=in_specs,
        out_specs=out_specs,
        out_shape=out_shape,
        scratch_shapes=[pltpu.VMEM((CTX_LEN + 2 * SUBLANES, LANES), F32),
                        pltpu.VMEM((SEQ + 2 * SUBLANES, LANES), F32),
                        pltpu.VMEM((n_all, LANES), F32), pltpu.VMEM((n_all, LANES), F32),
                        pltpu.VMEM((n_all, LANES), F32), pltpu.VMEM((n_all, LANES), F32)],
        compiler_params=_params(("arbitrary", "arbitrary")),
        name="rglru",
    )(*args)


def _cat(ref):
    return jnp.concatenate([ref[g] for g in range(NCB)], axis=1)


def _merge_body(x_ref, mod_ref, g2_ref, a_ref, b_ref, c_ref, ga_ref, gb_ref, gc_ref, wb_ref, wo_ref,
                xo_ref, h2_ref):
    oa = jnp.dot(_cat(a_ref), wb_ref[0], preferred_element_type=F32)
    ob = jnp.dot(_cat(b_ref), wb_ref[1], preferred_element_type=F32)
    oc = jnp.dot(_cat(c_ref), wb_ref[2], preferred_element_type=F32)
    mixed = (jax.nn.sigmoid(_cat(ga_ref).astype(F32)) * oa + jax.nn.sigmoid(_cat(gb_ref).astype(F32)) * ob
             + jax.nn.sigmoid(_cat(gc_ref).astype(F32)) * oc)
    y = jnp.dot(mixed.astype(BF16), wo_ref[...], preferred_element_type=F32)
    xn = x_ref[...] + mod_ref[0, 2:3, :] * y
    xo_ref[...] = xn
    ms = jnp.mean(xn * xn, axis=-1, keepdims=True)
    hn = xn * lax.rsqrt(ms + NORM_EPS) * g2_ref[...]
    h2_ref[...] = hn * (1.0 + mod_ref[0, 4:5, :]) + mod_ref[0, 3:4, :]


def _merge(x, mod, mod_row, norm2_g, ret, na, lru, proj, wb, wo, tm):
    t = x.shape[0]

    def cb(sec):
        return pl.BlockSpec((NCB, tm, LANES), lambda i: (sec, i, 0))

    tok = pl.BlockSpec((tm, D_MODEL), lambda i: (i, 0))
    shape = jax.ShapeDtypeStruct((t, D_MODEL), F32)
    return pl.pallas_call(
        _merge_body,
        grid=(t // tm,),
        in_specs=[tok,
                  pl.BlockSpec((1, 6, D_MODEL), lambda i: (mod_row(i), 0, 0)),
                  pl.BlockSpec((1, D_MODEL), lambda i: (0, 0)),
                  cb(0), cb(0), cb(0), cb(S_GA), cb(S_GB), cb(S_GC),
                  pl.BlockSpec((3, D_MODEL, D_MODEL), lambda i: (0, 0, 0)),
                  pl.BlockSpec((D_MODEL, D_MODEL), lambda i: (0, 0))],
        out_specs=[tok, tok],
        out_shape=[shape, shape],
        compiler_params=_params(("arbitrary",)),
        name="merge",
    )(x, mod, norm2_g.reshape(1, D_MODEL), ret, na, lru, proj, proj, proj, wb, wo)


def _router_body(h_ref, wr_ref, br_ref, init_ref, ids_ref, w_ref, within_ref, cnt_ref, run_ref):
    tm = h_ref.shape[0]

    @pl.when(pl.program_id(0) == 0)
    def _():
        run_ref[...] = init_ref[...]

    logits = lax.dot_general(wr_ref[...], h_ref[...], (((1,), (1,)), ((), ())), preferred_element_type=F32,
                             precision=lax.Precision.HIGHEST) + br_ref[...]
    eidx = lax.broadcasted_iota(jnp.int32, (N_EXPERTS, tm), 0).astype(F32)
    cur = logits
    sel = jnp.zeros((N_EXPERTS, tm), F32)
    vals, hots = [], []
    for _ in range(TOP_K):
        m = jnp.max(cur, axis=0, keepdims=True)
        idx = jnp.min(jnp.where(cur == m, eidx, float(N_EXPERTS)), axis=0, keepdims=True)
        hot = eidx == idx
        vals.append(m)
        hots.append(hot)
        sel = sel + hot.astype(F32)
        cur = jnp.where(hot, -jnp.inf, cur)
    exps = [jnp.exp(v - vals[0]) for v in vals]
    den = functools.reduce(jnp.add, exps)

    ti = lax.broadcasted_iota(jnp.int32, (tm, tm), 0)
    tj = lax.broadcasted_iota(jnp.int32, (tm, tm), 1)
    upper = (ti < tj).astype(BF16)
    before = jnp.dot(sel.astype(BF16), upper, preferred_element_type=F32) + run_ref[:, 0:1]
    for k in range(TOP_K):
        ids_ref[k:k + 1, :] = jnp.sum(jnp.where(hots[k], eidx, 0.0), axis=0, keepdims=True).astype(jnp.int32)
        w_ref[k:k + 1, :] = exps[k] / den
        within_ref[k:k + 1, :] = jnp.sum(jnp.where(hots[k], before, 0.0), axis=0, keepdims=True).astype(jnp.int32)
    run_ref[...] = run_ref[...] + jnp.sum(sel, axis=1, keepdims=True)
    cnt_ref[...] = run_ref[...]


def _router(h2, w_router_t, b_router, init_counts, tm):
    t = h2.shape[0]
    tok4 = pl.BlockSpec((TOP_K, tm), lambda i: (0, i))
    cnt = pl.BlockSpec((N_EXPERTS, LANES), lambda i: (0, 0))
    return pl.pallas_call(
        _router_body,
        grid=(t // tm,),
        in_specs=[pl.BlockSpec((tm, D_MODEL), lambda i: (i, 0)),
                  pl.BlockSpec((N_EXPERTS, D_MODEL), lambda i: (0, 0)),
                  pl.BlockSpec((N_EXPERTS, 1), lambda i: (0, 0)),
                  cnt],
        out_specs=[tok4, tok4, tok4, cnt],
        out_shape=[jax.ShapeDtypeStruct((TOP_K, t), jnp.int32), jax.ShapeDtypeStruct((TOP_K, t), F32),
                   jax.ShapeDtypeStruct((TOP_K, t), jnp.int32),
                   jax.ShapeDtypeStruct((N_EXPERTS, LANES), F32)],
        scratch_shapes=[pltpu.VMEM((N_EXPERTS, LANES), F32)],
        compiler_params=_params(("arbitrary",)),
        name="router_topk",
    )(h2, w_router_t, b_router.reshape(N_EXPERTS, 1), init_counts)


def _row_copy(src_ref, src_row, dst_ref, dst_row, sem):
    return pltpu.make_async_copy(src_ref.at[pl.ds(pl.multiple_of(src_row * SUBLANES, SUBLANES), SUBLANES)],
                                 dst_ref.at[pl.ds(pl.multiple_of(dst_row * SUBLANES, SUBLANES), SUBLANES)], sem)


def _dispatch_body(pos_ref, h_ref, xg_in_ref, xg_ref, slab_ref, sem):
    del xg_in_ref
    tm = h_ref.shape[0]
    for s in range(NCB):
        slab_ref[pl.ds(s, tm, stride=SUBLANES), :] = h_ref[:, s * LANES:(s + 1) * LANES]

    def start(t, _):
        for k in range(TOP_K):
            _row_copy(slab_ref, t, xg_ref, pos_ref[k, t], sem).start(priority=k % 2)
        return 0

    lax.fori_loop(0, tm, start, 0)

    def wait(t, _):
        for k in range(TOP_K):
            _row_copy(slab_ref, t, xg_ref, pos_ref[k, t], sem).wait()
        return 0

    lax.fori_loop(0, tm, wait, 0)


def _dispatch(h2, pos, xg, tm):
    t = h2.shape[0]
    return pl.pallas_call(
        _dispatch_body,
        grid=(t // tm,),
        in_specs=[pl.BlockSpec((TOP_K, tm), lambda i: (0, i), memory_space=pltpu.SMEM),
                  pl.BlockSpec((tm, D_MODEL), lambda i: (i, 0)),
                  pl.BlockSpec(memory_space=pl.ANY)],
        out_specs=pl.BlockSpec(memory_space=pl.ANY),
        out_shape=jax.ShapeDtypeStruct(xg.shape, F32),
        scratch_shapes=[pltpu.VMEM((tm * SUBLANES, LANES), F32), pltpu.SemaphoreType.DMA],
        input_output_aliases={2: 0},
        compiler_params=_params(("arbitrary",)),
        name="moe_dispatch",
    )(pos, h2, xg)


DEINT_COLS = 2 * LANES
DEINT_ROWS = 512


def _deinterleave_matrix():
    src = np.arange(DEINT_COLS)[:, None]
    dst = np.arange(DEINT_COLS)[None, :]
    pick = np.where(dst < LANES, src == 2 * dst, src == 2 * (dst - LANES) + 1)
    return jnp.asarray(pick, BF16)


def _deint_body(w_ref, p_ref, g_ref, l_ref):
    for cidx in range(2 * D_MODEL // DEINT_COLS):
        chunk = w_ref[0, 0, :, cidx * DEINT_COLS:(cidx + 1) * DEINT_COLS].astype(BF16)
        r = jnp.dot(chunk, p_ref[...], preferred_element_type=F32)
        g_ref[0, :, cidx * LANES:(cidx + 1) * LANES] = r[:, :LANES].astype(BF16)
        l_ref[0, :, cidx * LANES:(cidx + 1) * LANES] = r[:, LANES:].astype(BF16)


def _split_up_projection(w_up, layer):
    out = pl.BlockSpec((1, DEINT_ROWS, D_MODEL), lambda e, i: (e, i, 0))
    shape = jax.ShapeDtypeStruct((N_EXPERTS, D_MODEL, D_MODEL), BF16)
    return pl.pallas_call(
        _deint_body,
        grid=(N_EXPERTS, D_MODEL // DEINT_ROWS),
        in_specs=[pl.BlockSpec((1, 1, DEINT_ROWS, 2 * D_MODEL), lambda e, i: (layer, e, i, 0)),
                  pl.BlockSpec((DEINT_COLS, DEINT_COLS), lambda e, i: (0, 0))],
        out_specs=[out, out],
        out_shape=[shape, shape],
        compiler_params=_params(("arbitrary", "arbitrary")),
        name="split_up_projection",
    )(w_up, _deinterleave_matrix())


def _expert_body(te_ref, nu_ref, x_ref, wg_ref, bg_ref, wl_ref, bl_ref, wd_ref, bd_ref, y_ref):
    del te_ref
    tm = x_ref.shape[0] // SUBLANES

    @pl.when(pl.program_id(0) < nu_ref[0])
    def _():
        x = jnp.concatenate([x_ref[pl.ds(s, tm, stride=SUBLANES), :] for s in range(NCB)], axis=1).astype(BF16)
        glu = jnp.minimum(jnp.dot(x, wg_ref[0], preferred_element_type=F32) + bg_ref[0], SWIGLU_LIMIT)
        lin = jnp.clip(jnp.dot(x, wl_ref[0], preferred_element_type=F32) + bl_ref[0], -SWIGLU_LIMIT, SWIGLU_LIMIT)
        act = glu * jax.nn.sigmoid(SWIGLU_ALPHA * glu) * (lin + 1.0)
        y = jnp.dot(act.astype(BF16), wd_ref[0], preferred_element_type=F32) + bd_ref[0]
        for s in range(NCB):
            y_ref[pl.ds(s, tm, stride=SUBLANES), :] = y[:, s * LANES:(s + 1) * LANES]

    @pl.when(pl.program_id(0) >= nu_ref[0])
    def _():
        y_ref[...] = jnp.zeros_like(y_ref)


def _experts(xg, tile_expert, n_used, wg, bg, wl, bl, wd, bd):
    tm = EXPERT_TILE
    n_tiles = xg.shape[0] // (tm * SUBLANES)

    def rows(m, te, nu):
        return (jnp.maximum(jnp.minimum(m, nu[0] - 1), 0), 0)

    def per_expert(shape):
        return pl.BlockSpec((1,) + shape, lambda m, te, nu: (te[m], 0, 0))

    grid_spec = pltpu.PrefetchScalarGridSpec(
        num_scalar_prefetch=2,
        grid=(n_tiles,),
        in_specs=[pl.BlockSpec((tm * SUBLANES, LANES), rows),
                  per_expert((D_MODEL, D_MODEL)), per_expert((1, D_MODEL)),
                  per_expert((D_MODEL, D_MODEL)), per_expert((1, D_MODEL)),
                  per_expert((D_MODEL, D_MODEL)), per_expert((1, D_MODEL))],
        out_specs=pl.BlockSpec((tm * SUBLANES, LANES), lambda m, te, nu: (m, 0)),
    )
    return pl.pallas_call(
        _expert_body,
        grid_spec=grid_spec,
        out_shape=jax.ShapeDtypeStruct(xg.shape, F32),
        compiler_params=_params(("arbitrary",)),
        name="moe_experts",
    )(tile_expert, n_used, xg, wg, bg, wl, bl, wd, bd)


def _combine_body(pos_ref, pos_next_ref, x_ref, w_ref, mod_ref, fg_ref, yg_ref, o_ref, ybuf_ref, sems, *,
                  final_norm):
    tm = x_ref.shape[0]
    i = pl.program_id(0)
    n = pl.num_programs(0)

    def for_each_row_copy(p_ref, slot, fn):
        def body(t, _):
            for k in range(TOP_K):
                fn(_row_copy(yg_ref, p_ref[k, t], ybuf_ref.at[slot, k], t, sems.at[slot]), k)
            return 0

        lax.fori_loop(0, tm, body, 0)

    def step(slot):
        @pl.when(i == 0)
        def _():
            for_each_row_copy(pos_ref, slot, lambda cp, k: cp.start(priority=k % 2))

        @pl.when(i + 1 < n)
        def _():
            for_each_row_copy(pos_next_ref, 1 - slot, lambda cp, k: cp.start(priority=k % 2))

        for_each_row_copy(pos_ref, slot, lambda cp, k: cp.wait())
        acc = jnp.zeros((tm, D_MODEL), F32)
        for k in range(TOP_K):
            yk = jnp.concatenate([ybuf_ref[slot, k, pl.ds(s, tm, stride=SUBLANES), :] for s in range(NCB)], axis=1)
            acc = acc + w_ref[:, k:k + 1] * yk
        xn = x_ref[...] + mod_ref[0, 5:6, :] * acc
        if final_norm:
            ms = jnp.mean(xn * xn, axis=-1, keepdims=True)
            xn = xn * lax.rsqrt(ms + NORM_EPS) * fg_ref[...]
        o_ref[...] = xn

    for slot in range(2):
        pl.when(i % 2 == slot)(functools.partial(step, slot))


def _combine(x, w_tok, pos, yg, mod, mod_row, final_g, tm, final_norm):
    t = x.shape[0]
    n = t // tm
    return pl.pallas_call(
        functools.partial(_combine_body, final_norm=final_norm),
        grid=(n,),
        in_specs=[pl.BlockSpec((TOP_K, tm), lambda i: (0, i), memory_space=pltpu.SMEM),
                  pl.BlockSpec((TOP_K, tm), lambda i: (0, jnp.minimum(i + 1, n - 1)), memory_space=pltpu.SMEM),
                  pl.BlockSpec((tm, D_MODEL), lambda i: (i, 0)),
                  pl.BlockSpec((tm, TOP_K), lambda i: (i, 0)),
                  pl.BlockSpec((1, 6, D_MODEL), lambda i: (mod_row(i), 0, 0)),
                  pl.BlockSpec((1, D_MODEL), lambda i: (0, 0)),
                  pl.BlockSpec(memory_space=pl.ANY)],
        out_specs=pl.BlockSpec((tm, D_MODEL), lambda i: (i, 0)),
        out_shape=jax.ShapeDtypeStruct((t, D_MODEL), F32),
        scratch_shapes=[pltpu.VMEM((2, TOP_K, tm * SUBLANES, LANES), F32), pltpu.SemaphoreType.DMA((2,))],
        compiler_params=_params(("arbitrary",)),
        name="moe_combine",
    )(pos, pos, x, w_tok, mod, final_g.reshape(1, D_MODEL), yg)


def _moe(streams, mod, w_router, b_router, w_up_all, layer, b_up, w_down, b_down, final_g, final_norm):
    te = EXPERT_TILE
    w_router_t = w_router.T
    counts = jnp.zeros((N_EXPERTS, LANES), F32)
    routed = []
    for _, h2, _ in streams:
        ids, w, within, counts = _router(h2, w_router_t, b_router, counts, min(512, h2.shape[0]))
        routed.append((ids, w, within))
    total = sum(h2.shape[0] for _, h2, _ in streams)
    cnt = counts[:, 0].astype(jnp.int32)
    padded = ((cnt + te - 1) // te) * te
    ends = jnp.cumsum(padded)
    starts = ends - padded
    n_tiles = total * TOP_K // te + N_EXPERTS
    tile_ids = jnp.arange(n_tiles, dtype=jnp.int32)
    tile_expert = jnp.minimum(jnp.sum((ends // te)[None, :] <= tile_ids[:, None], axis=1),
                              N_EXPERTS - 1).astype(jnp.int32)
    n_used = (ends[-1:] // te).astype(jnp.int32)
    expert_ids = jnp.arange(N_EXPERTS, dtype=jnp.int32)[:, None, None]

    wg, wl = _split_up_projection(w_up_all, layer)
    bg = b_up[:, None, 0::2]
    bl = b_up[:, None, 1::2]

    xg = jnp.zeros((n_tiles * te * SUBLANES, LANES), F32)
    positions = []
    for (_, h2, _), (ids, _, within) in zip(streams, routed):
        pos = within + jnp.sum(jnp.where(ids[None] == expert_ids, starts[:, None, None], 0), axis=0)
        positions.append(pos)
        xg = _dispatch(h2, pos, xg, ROW_DMA_TILE)
    yg = _experts(xg, tile_expert, n_used, wg, bg, wl, bl, w_down.astype(BF16), b_down[:, None, :])
    return [_combine(x, w.T, pos, yg, mod, mod_row, final_g, ROW_DMA_TILE, final_norm)
            for (x, _, mod_row), (_, w, _), pos in zip(streams, routed, positions)]


def _rope_tables():
    tok = jnp.arange(SEQ)
    nf = 32
    inv = ROPE_BASE ** (-jnp.arange(nf, dtype=F32) / nf)
    ang_r = (tok // GRID_W).astype(F32)[:, None] * inv
    ang_c = (tok % GRID_W).astype(F32)[:, None] * inv
    cos = jnp.concatenate([jnp.cos(ang_r)] * 2 + [jnp.cos(ang_c)] * 2, axis=-1)
    sin = jnp.concatenate([-jnp.sin(ang_r), jnp.sin(ang_r), -jnp.sin(ang_c), jnp.sin(ang_c)], axis=-1)
    return cos, sin


def kernel(x, c, ctx, c_ctx, w_mod, b_mod, norm1_g, norm2_g, w_mix_in, ret_decay_fwd, ret_decay_bwd, na_rel_bias, lru_conv_w, lru_conv_b, lru_gate_a_w, lru_gate_a_b, lru_gate_x_w, lru_gate_x_b, lru_lambda, w_branch, w_mix_out, w_router, b_router, w_expert_up, b_expert_up, w_expert_down, b_expert_down, final_norm_g):
    batch = x.shape[0]
    depth = w_mod.shape[0]
    assert x.shape[1:] == (SEQ, D_MODEL) and ctx.shape[1:] == (CTX_LEN, D_MODEL)
    tm_l = TOKEN_TILE
    tm_c = min(TOKEN_TILE, batch * CTX_LEN)
    tm_ml = MERGE_TILE
    tm_mc = min(MERGE_TILE, batch * CTX_LEN)
    assert SEQ % tm_l == 0 and (batch * CTX_LEN) % tm_c == 0 and (batch * CTX_LEN) % tm_mc == 0

    def lat_row(tile_rows):
        return lambda i: i // (SEQ // tile_rows)

    def ctx_row(tile_rows):
        del tile_rows
        return lambda i: batch

    xl = x.reshape(batch * SEQ, D_MODEL)
    xc = ctx.reshape(batch * CTX_LEN, D_MODEL)
    mp = -(-(batch + 1) // SUBLANES) * SUBLANES
    cc = jnp.zeros((mp, D_MODEL), F32).at[:batch].set(c).at[batch].set(c_ctx)
    cos, sin_signed = _rope_tables()

    for l in range(depth):
        last = l == depth - 1
        need_ctx = not last
        mod = _modulation(cc, w_mod[l], b_mod[l])
        w_in = w_mix_in[l].astype(BF16)
        proj_l = _input_projection(xl, mod, lat_row(tm_l), norm1_g[l], w_in, tm_l)
        proj_c = _input_projection(xc, mod, ctx_row(tm_c), norm1_g[l], w_in, tm_c)
        lg = jnp.stack([jax.nn.log_sigmoid(ret_decay_fwd[l].astype(F32)),
                        jax.nn.log_sigmoid(ret_decay_bwd[l].astype(F32))])
        ret = _retention(proj_l, proj_c, lg, cos, sin_signed, batch, need_ctx)
        na = _neighbourhood(proj_l, proj_c, _na_bias_table(na_rel_bias[l]), batch, need_ctx)
        lru = _rglru(proj_l, proj_c, lru_conv_w[l], lru_conv_b[l], lru_gate_a_w[l].astype(BF16), lru_gate_a_b[l],
                     lru_gate_x_w[l].astype(BF16), lru_gate_x_b[l], lru_lambda[l], batch, need_ctx)
        wb = w_branch[l].astype(BF16)
        wo = w_mix_out[l].astype(BF16)
        xl, h2l = _merge(xl, mod, lat_row(tm_ml), norm2_g[l], ret[0], na[0], lru[0], proj_l, wb, wo, tm_ml)
        streams = [(xl, h2l, lat_row(ROW_DMA_TILE))]
        if need_ctx:
            xc, h2c = _merge(xc, mod, ctx_row(tm_mc), norm2_g[l], ret[1], na[1], lru[1], proj_c, wb, wo, tm_mc)
            streams = [(xc, h2c, ctx_row(ROW_DMA_TILE))] + streams
        outs = _moe(streams, mod, w_router[l], b_router[l], w_expert_up, l, b_expert_up[l], w_expert_down[l],
                    b_expert_down[l], final_norm_g, last)
        xl = outs[-1]
        if need_ctx:
            xc = outs[0]
    return xl.reshape(batch, SEQ, D_MODEL)
```

```python
import functools

import numpy as np
import jax
import jax.numpy as jnp
from jax import lax
from jax.experimental import pallas as pl
from jax.experimental.pallas import tpu as pltpu

F32 = jnp.float32
BF16 = jnp.bfloat16

D_MODEL = 1024
SEQ = 2048
CTX_LEN = 256
GRID_W = 64
N_ROWS = SEQ // GRID_W
LANES = 128
SUBLANES = 8
NCB = D_MODEL // LANES
N_SEC = 12
RET_HEADS = 8
RET_CHUNK = 256
NA_HEADS = 16
NA_KH = 8
NA_KW = 16
NA_GROUP = 4
NA_GROUPS_PER_STEP = 2
NA_SLAB = 12
LRU_SCAN_TILES = 4
LRU_BLOCKS = 8
LRU_C = 8.0
N_EXPERTS = 32
TOP_K = 4
SWIGLU_LIMIT = 7.0
SWIGLU_ALPHA = 1.702
ROPE_BASE = 10000.0
NORM_EPS = 1e-6
NEG_BIG = -1e30
TOKEN_TILE = 1024
MERGE_TILE = 512
ROW_DMA_TILE = 256
ROW_DMA_UNROLL = 4
EXPERT_TILE = 512
VMEM_LIMIT = 56 * 1024 * 1024

S_RQ, S_RK, S_RV, S_RG, S_NQ, S_NK, S_NV, S_LX, S_LY, S_GA, S_GB, S_GC = range(12)


def _nt_dot(a, b):
    return lax.dot_general(a, b, (((1,), (1,)), ((), ())), preferred_element_type=F32)


def _tn_dot(a, b):
    return lax.dot_general(a, b, (((0,), (0,)), ((), ())), preferred_element_type=F32)


def _params(sem, vmem=VMEM_LIMIT):
    return pltpu.CompilerParams(dimension_semantics=sem, vmem_limit_bytes=vmem)


def _mod_body(c_ref, w_ref, b_ref, o_ref):
    c = c_ref[...]
    s = c * jax.nn.sigmoid(c)
    o_ref[...] = jnp.dot(s, w_ref[...], preferred_element_type=F32,
                         precision=lax.Precision.HIGHEST) + b_ref[...]


def _modulation(cc, w_mod, b_mod):
    mp = cc.shape[0]
    out = pl.pallas_call(
        _mod_body,
        grid=(6,),
        in_specs=[pl.BlockSpec((mp, D_MODEL), lambda j: (0, 0)),
                  pl.BlockSpec((D_MODEL, D_MODEL), lambda j: (0, j)),
                  pl.BlockSpec((1, D_MODEL), lambda j: (0, j))],
        out_specs=pl.BlockSpec((mp, D_MODEL), lambda j: (0, j)),
        out_shape=jax.ShapeDtypeStruct((mp, 6 * D_MODEL), F32),
        compiler_params=_params(("arbitrary",)),
        name="adaln_mod",
    )(cc, w_mod, b_mod.reshape(1, 6 * D_MODEL))
    return out.reshape(mp, 6, D_MODEL)


def _inproj_body(x_ref, mod_ref, g_ref, w_ref, o_ref, h_ref):
    @pl.when(pl.program_id(1) == 0)
    def _():
        x = x_ref[...]
        ms = jnp.mean(x * x, axis=-1, keepdims=True)
        y = x * lax.rsqrt(ms + NORM_EPS) * g_ref[...]
        h_ref[...] = (y * (1.0 + mod_ref[0, 1:2, :]) + mod_ref[0, 0:1, :]).astype(BF16)

    acc = jnp.dot(h_ref[...], w_ref[...], preferred_element_type=F32)
    for g in range(NCB):
        o_ref[g] = acc[:, g * LANES:(g + 1) * LANES].astype(BF16)


def _input_projection(x, mod, mod_row, norm_g, w_in, tm):
    t = x.shape[0]
    return pl.pallas_call(
        _inproj_body,
        grid=(t // tm, N_SEC),
        in_specs=[pl.BlockSpec((tm, D_MODEL), lambda i, j: (i, 0)),
                  pl.BlockSpec((1, 6, D_MODEL), lambda i, j: (mod_row(i), 0, 0)),
                  pl.BlockSpec((1, D_MODEL), lambda i, j: (0, 0)),
                  pl.BlockSpec((D_MODEL, D_MODEL), lambda i, j: (0, j))],
        out_specs=pl.BlockSpec((NCB, tm, LANES), lambda i, j: (j, i, 0)),
        out_shape=jax.ShapeDtypeStruct((N_SEC * NCB, t, LANES), BF16),
        scratch_shapes=[pltpu.VMEM((tm, D_MODEL), BF16)],
        compiler_params=_params(("arbitrary", "arbitrary")),
        name="in_proj",
    )(x, mod, norm_g.reshape(1, D_MODEL), w_in)


def _rope(x, cos, sin_signed, first_half):
    partner = jnp.where(first_half, pltpu.roll(x, LANES - 32, 1), pltpu.roll(x, 32, 1))
    return x * cos + partner * sin_signed


def _head_rms_gate(o, gate):
    o = o * lax.rsqrt(jnp.mean(o * o, axis=-1, keepdims=True) + NORM_EPS)
    return o * (gate * jax.nn.sigmoid(gate))


def _ret_body(*refs, need_ctx):
    if need_ctx:
        (lg_ref, q_ref, k_ref, v_ref, rg_ref, kc_ref, vc_ref, cos_ref, sin_ref, qc_ref, rgc_ref,
         o_ref, oc_ref, kr_ref, sb_ref, dc_ref) = refs
    else:
        (lg_ref, q_ref, k_ref, v_ref, rg_ref, kc_ref, vc_ref, cos_ref, sin_ref,
         o_ref, kr_ref, sb_ref, dc_ref) = refs
    h = pl.program_id(1)
    c = RET_CHUNK
    n_chunks = SEQ // c
    lgf = lg_ref[0, h]
    lgb = lg_ref[1, h]
    ks = float(LANES) ** -0.5

    ia = lax.broadcasted_iota(jnp.int32, (c, 1), 0).astype(F32)
    ii = lax.broadcasted_iota(jnp.int32, (c, c), 0)
    jj = lax.broadcasted_iota(jnp.int32, (c, c), 1)
    diff = (ii - jj).astype(F32)
    dc_ref[...] = jnp.where(diff >= 0, jnp.exp(diff * lgf), jnp.exp(-diff * lgb))
    w_f = jnp.exp((c - 1.0 - ia) * lgf)
    w_b = jnp.exp(ia * lgb)
    x_f = jnp.exp((ia + 1.0) * lgf)
    x_b = jnp.exp((c - ia) * lgb)
    g_f = jnp.exp(c * lgf)
    g_b = jnp.exp(c * lgb)
    lane = lax.broadcasted_iota(jnp.int32, (1, LANES), 1)
    first_half = (lane % 64) < 32

    kc = kc_ref[0].astype(F32)
    vc = vc_ref[0]
    sf0 = _tn_dot((kc * w_f).astype(BF16), vc)
    sb0 = _tn_dot((kc * w_b).astype(BF16), vc)

    def bwd_body(i, sb):
        n = n_chunks - 1 - i
        r0 = pl.multiple_of(n * c, c)
        sb_ref[n] = sb
        k = _rope(k_ref[0, pl.ds(r0, c), :].astype(F32), cos_ref[pl.ds(r0, c), :], sin_ref[pl.ds(r0, c), :],
                  first_half)
        kr_ref[pl.ds(r0, c), :] = k.astype(BF16)
        ub = _tn_dot((k * w_b).astype(BF16), v_ref[0, pl.ds(r0, c), :])
        return ub + g_b * sb

    lax.fori_loop(0, n_chunks, bwd_body, sb0, unroll=True)

    def fwd_body(n, sf):
        r0 = pl.multiple_of(n * c, c)
        q = _rope(q_ref[0, pl.ds(r0, c), :].astype(F32), cos_ref[pl.ds(r0, c), :], sin_ref[pl.ds(r0, c), :],
                  first_half)
        k = kr_ref[pl.ds(r0, c), :]
        v = v_ref[0, pl.ds(r0, c), :]
        s = _nt_dot(q.astype(BF16), k)
        o = jnp.dot((s * dc_ref[...]).astype(BF16), v, preferred_element_type=F32)
        o = o + jnp.dot((q * x_f).astype(BF16), sf.astype(BF16), preferred_element_type=F32)
        o = o + jnp.dot((q * x_b).astype(BF16), sb_ref[n].astype(BF16), preferred_element_type=F32)
        o = _head_rms_gate(o * ks, rg_ref[0, pl.ds(r0, c), :].astype(F32))
        o_ref[0, pl.ds(r0, c), :] = o.astype(BF16)
        uf = _tn_dot((k.astype(F32) * w_f).astype(BF16), v)
        return uf + g_f * sf

    lax.fori_loop(0, n_chunks, fwd_body, sf0, unroll=True)

    if need_ctx:
        s = _nt_dot(qc_ref[0], kc_ref[0])
        o = jnp.dot((s * dc_ref[...]).astype(BF16), vc, preferred_element_type=F32)
        oc_ref[0] = _head_rms_gate(o * ks, rgc_ref[0].astype(F32)).astype(BF16)


def _retention(proj_l, proj_c, lg, cos, sin_signed, batch, need_ctx):
    def lat(sec):
        return pl.BlockSpec((1, SEQ, LANES), lambda b, h: (sec * NCB + h, b, 0))

    def ctx(sec):
        return pl.BlockSpec((1, CTX_LEN, LANES), lambda b, h: (sec * NCB + h, b, 0))

    table = pl.BlockSpec((SEQ, LANES), lambda b, h: (0, 0))
    in_specs = [pl.BlockSpec(memory_space=pltpu.SMEM),
                lat(S_RQ), lat(S_RK), lat(S_RV), lat(S_RG), ctx(S_RK), ctx(S_RV), table, table]
    args = [lg, proj_l, proj_l, proj_l, proj_l, proj_c, proj_c, cos, sin_signed]
    out_specs = [pl.BlockSpec((1, SEQ, LANES), lambda b, h: (h, b, 0))]
    out_shape = [jax.ShapeDtypeStruct((RET_HEADS, batch * SEQ, LANES), BF16)]
    if need_ctx:
        in_specs += [ctx(S_RQ), ctx(S_RG)]
        args += [proj_c, proj_c]
        out_specs.append(pl.BlockSpec((1, CTX_LEN, LANES), lambda b, h: (h, b, 0)))
        out_shape.append(jax.ShapeDtypeStruct((RET_HEADS, batch * CTX_LEN, LANES), BF16))
    return pl.pallas_call(
        functools.partial(_ret_body, need_ctx=need_ctx),
        grid=(batch, RET_HEADS),
        in_specs=in_specs,
        out_specs=out_specs,
        out_shape=out_shape,
        scratch_shapes=[pltpu.VMEM((SEQ, LANES), BF16),
                        pltpu.VMEM((SEQ // RET_CHUNK, LANES, LANES), F32),
                        pltpu.VMEM((RET_CHUNK, RET_CHUNK), F32)],
        compiler_params=_params(("arbitrary", "arbitrary")),
        name="retention",
    )(*args)


def _na_softmax_pv(s_parts, v_parts):
    m = functools.reduce(jnp.maximum, [jnp.max(s, axis=-1, keepdims=True) for s in s_parts])
    p_parts = [jnp.exp(s - m) for s in s_parts]
    den = functools.reduce(jnp.add, [jnp.sum(p, axis=-1, keepdims=True) for p in p_parts])
    o = functools.reduce(jnp.add, [jnp.dot(p.astype(BF16), v, preferred_element_type=F32)
                                   for p, v in zip(p_parts, v_parts)])
    return o / den


def _split_heads(q, low):
    zero = jnp.zeros_like(q)
    return jnp.concatenate([jnp.where(low, q, zero), jnp.where(low, zero, q)], axis=0)


def _na_body(*refs, need_ctx):
    if need_ctx:
        q_ref, k_ref, v_ref, kc_ref, vc_ref, bias_ref, qc_ref, o_ref, oc_ref = refs
    else:
        q_ref, k_ref, v_ref, kc_ref, vc_ref, bias_ref, o_ref = refs
    lane = lax.broadcasted_iota(jnp.int32, (1, LANES), 1)
    low = lane < 64
    scale = 64.0 ** -0.5
    kc = kc_ref[0]
    vc = vc_ref[0]
    n_q = NA_GROUP * GRID_W
    n_groups = N_ROWS // NA_GROUP

    def group_scores(gi):
        r0 = jnp.clip(gi * NA_GROUP - NA_KH // 2, 0, N_ROWS - NA_SLAB)
        case = jnp.where(gi == 0, 0, jnp.where(gi == n_groups - 1, 2, 1))
        q0 = pl.multiple_of(gi * n_q, n_q)
        k0 = pl.multiple_of(r0 * GRID_W, GRID_W)
        q = q_ref[0, pl.ds(q0, n_q), :] * scale
        kw = k_ref[0, pl.ds(k0, NA_SLAB * GRID_W), :]
        vw = v_ref[0, pl.ds(k0, NA_SLAB * GRID_W), :]
        zero = jnp.zeros_like(q)
        units = []
        for hd in range(2):
            qh = jnp.where(low, q, zero) if hd == 0 else jnp.where(low, zero, q)
            units.append(([_nt_dot(qh, kw) + bias_ref[hd, case], _nt_dot(qh, kc)], [vw, vc]))
        return q0, units

    def group_body(i, carry):
        staged = [group_scores(i * NA_GROUPS_PER_STEP + j) for j in range(NA_GROUPS_PER_STEP)]
        for q0, units in staged:
            o0, o1 = [_na_softmax_pv(s_parts, v_parts) for s_parts, v_parts in units]
            o_ref[0, pl.ds(q0, n_q), :] = jnp.where(low, o0, o1).astype(BF16)
        return carry

    lax.fori_loop(0, n_groups // NA_GROUPS_PER_STEP, group_body, 0)

    if need_ctx:
        q2 = _split_heads(qc_ref[0] * scale, low)
        o = _na_softmax_pv([_nt_dot(q2, kc)], [vc])
        oc_ref[0] = jnp.where(low, o[:CTX_LEN], o[CTX_LEN:]).astype(BF16)


def _na_bias_table(rpb):
    w = np.arange(GRID_W)
    c0 = np.clip(w - NA_KW // 2, 0, GRID_W - NA_KW)
    kcol = np.arange(GRID_W)
    in_win = (kcol[None, :] >= c0[:, None]) & (kcol[None, :] < c0[:, None] + NA_KW)
    dc = kcol[None, :] - w[:, None] + (NA_KW - 1)
    pick = ((dc[None] == np.arange(2 * NA_KW - 1)[:, None, None]) & in_win[None]).astype(np.float32)
    band = jnp.einsum("hrc,cwk->hrwk", rpb.astype(F32), pick, precision=lax.Precision.HIGHEST)
    band = jnp.where(in_win[None, None], band, NEG_BIG)

    def rows_of_group(gi):
        r = gi * NA_GROUP + np.arange(NA_GROUP)[:, None]
        krow = np.clip(gi * NA_GROUP - NA_KH // 2, 0, N_ROWS - NA_SLAB) + np.arange(NA_SLAB)[None, :]
        r0 = np.clip(r - NA_KH // 2, 0, N_ROWS - NA_KH)
        return krow - r + (NA_KH - 1), (krow >= r0) & (krow < r0 + NA_KH)

    n_groups = N_ROWS // NA_GROUP
    cases = [rows_of_group(0), rows_of_group(1), rows_of_group(n_groups - 1)]
    for gi in range(1, n_groups - 1):
        dr_i, ok_i = rows_of_group(gi)
        assert (ok_i == cases[1][1]).all() and (dr_i[ok_i] == cases[1][0][cases[1][1]]).all()
    dr = np.stack([np.where(ok, d, 0) for d, ok in cases])
    ok = np.stack([ok for _, ok in cases])
    tab = jnp.take(band, jnp.asarray(dr.reshape(-1), jnp.int32), axis=1)
    tab = tab.reshape(NA_HEADS, 3, NA_GROUP, NA_SLAB, GRID_W, GRID_W)
    tab = jnp.where(ok[None, :, :, :, None, None], tab, NEG_BIG)
    return tab.transpose(0, 1, 2, 4, 3, 5).reshape(NA_HEADS, 3, NA_GROUP * GRID_W, NA_SLAB * GRID_W)


def _neighbourhood(proj_l, proj_c, bias, batch, need_ctx):
    n_pairs = NA_HEADS // 2

    def lat(sec):
        return pl.BlockSpec((1, SEQ, LANES), lambda p, b: (sec * NCB + p, b, 0))

    def ctx(sec):
        return pl.BlockSpec((1, CTX_LEN, LANES), lambda p, b: (sec * NCB + p, b, 0))

    in_specs = [lat(S_NQ), lat(S_NK), lat(S_NV), ctx(S_NK), ctx(S_NV),
                pl.BlockSpec((2, 3, NA_GROUP * GRID_W, NA_SLAB * GRID_W), lambda p, b: (p, 0, 0, 0))]
    args = [proj_l, proj_l, proj_l, proj_c, proj_c, bias]
    out_specs = [pl.BlockSpec((1, SEQ, LANES), lambda p, b: (p, b, 0))]
    out_shape = [jax.ShapeDtypeStruct((n_pairs, batch * SEQ, LANES), BF16)]
    if need_ctx:
        in_specs.append(ctx(S_NQ))
        args.append(proj_c)
        out_specs.append(pl.BlockSpec((1, CTX_LEN, LANES), lambda p, b: (p, b, 0)))
        out_shape.append(jax.ShapeDtypeStruct((n_pairs, batch * CTX_LEN, LANES), BF16))
    return pl.pallas_call(
        functools.partial(_na_body, need_ctx=need_ctx),
        grid=(n_pairs, batch),
        in_specs=in_specs,
        out_specs=out_specs,
        out_shape=out_shape,
        compiler_params=_params(("arbitrary", "arbitrary")),
        name="neighbourhood_attention",
    )(*args)


def _scan_tile(a, b, row, reverse):
    for sh in (1, 2, 4):
        if reverse:
            a_s = pltpu.roll(a, SUBLANES - sh, 0)
            b_s = pltpu.roll(b, SUBLANES - sh, 0)
            m = row < SUBLANES - sh
        else:
            a_s = pltpu.roll(a, sh, 0)
            b_s = pltpu.roll(b, sh, 0)
            m = row >= sh
        b = jnp.where(m, a * b_s + b, b)
        a = jnp.where(m, a * a_s, a)
    return a, b


def _lru_body(*refs, need_ctx):
    if need_ctx:
        (xc_ref, xl_ref, yl_ref, cw_ref, cb_ref, wa_ref, ba_ref, wx_ref, bx_ref, lam_ref, yc_ref,
         o_ref, oc_ref, xpc_ref, xpl_ref, af_ref, bf_ref, ab_ref, bb_ref) = refs
    else:
        (xc_ref, xl_ref, yl_ref, cw_ref, cb_ref, wa_ref, ba_ref, wx_ref, bx_ref, lam_ref,
         o_ref, xpc_ref, xpl_ref, af_ref, bf_ref, ab_ref, bb_ref) = refs
    pad = SUBLANES
    zpad = jnp.zeros((pad, LANES), F32)
    xpc_ref[0:pad, :] = zpad
    xpc_ref[pad:pad + CTX_LEN, :] = xc_ref[0].astype(F32)
    xpc_ref[pad + CTX_LEN:pad + CTX_LEN + pad, :] = zpad
    xpl_ref[0:pad, :] = zpad
    xpl_ref[pad:pad + SEQ, :] = xl_ref[0].astype(F32)
    xpl_ref[pad + SEQ:pad + SEQ + pad, :] = zpad

    cw = cw_ref[...]
    cb = cb_ref[...]
    a_refs = (af_ref, ab_ref)
    b_refs = (bf_ref, bb_ref)
    chunk = CTX_LEN

    def pointwise(xp_ref, t0, dst0):
        taps = [xp_ref[pad + t0 - 1 + j:pad + t0 - 1 + j + chunk, :] for j in range(4)]
        u = cw[0:1] * taps[0] + cw[1:2] * taps[1] + cw[2:3] * taps[2] + cw[3:4] * taps[3] + cb
        ub = u.astype(BF16)
        for d in range(2):
            r = jax.nn.sigmoid(jnp.dot(ub, wa_ref[d, 0], preferred_element_type=F32) + ba_ref[d:d + 1, :])
            i = jax.nn.sigmoid(jnp.dot(ub, wx_ref[d, 0], preferred_element_type=F32) + bx_ref[d:d + 1, :])
            log_a = (-LRU_C) * r * jax.nn.softplus(-lam_ref[d:d + 1, :])
            a = jnp.exp(log_a)
            a_refs[d][dst0:dst0 + chunk, :] = a
            b_refs[d][dst0:dst0 + chunk, :] = jnp.sqrt((1.0 - a) * (1.0 + a)) * (i * u)

    pointwise(xpc_ref, 0, 0)
    for cidx in range(SEQ // chunk):
        pointwise(xpl_ref, cidx * chunk, CTX_LEN + cidx * chunk)

    row = lax.broadcasted_iota(jnp.int32, (SUBLANES, LANES), 0)
    n_ctx_tiles = CTX_LEN // SUBLANES
    n_all_tiles = (CTX_LEN + SEQ) // SUBLANES

    def scan_group(jf, jb, carry):
        hf, hb = carry
        rows_f = [pl.multiple_of((jf + g) * SUBLANES, SUBLANES) for g in range(LRU_SCAN_TILES)]
        rows_b = [pl.multiple_of((jb - g) * SUBLANES, SUBLANES) for g in range(LRU_SCAN_TILES)]
        scans_f = [_scan_tile(af_ref[pl.ds(r, SUBLANES), :], bf_ref[pl.ds(r, SUBLANES), :], row, False)
                   for r in rows_f]
        scans_b = [_scan_tile(ab_ref[pl.ds(r, SUBLANES), :], bb_ref[pl.ds(r, SUBLANES), :], row, True)
                   for r in rows_b]
        out_f, out_b = [], []
        for (af, bf), (ab, bb) in zip(scans_f, scans_b):
            tf = bf + af * hf
            tb = bb + ab * hb
            out_f.append(tf)
            out_b.append(tb)
            hf, hb = tf[SUBLANES - 1:SUBLANES, :], tb[0:1, :]
        for r, t in zip(rows_f, out_f):
            bf_ref[pl.ds(r, SUBLANES), :] = t
        for r, t in zip(rows_b, out_b):
            bb_ref[pl.ds(r, SUBLANES), :] = t
        return hf, hb

    g = LRU_SCAN_TILES
    h0 = jnp.zeros((1, LANES), F32)
    carry = lax.fori_loop(0, n_ctx_tiles // g,
                          lambda i, cr: scan_group(i * g, n_ctx_tiles - 1 - i * g, cr), (h0, h0))
    lax.fori_loop(0, (n_all_tiles - n_ctx_tiles) // g,
                  lambda i, cr: scan_group(n_ctx_tiles + i * g, n_all_tiles - 1 - i * g, cr), carry)

    def out_body(cidx, _):
        r0 = pl.multiple_of(cidx * chunk, chunk)
        hsum = bf_ref[pl.ds(CTX_LEN + r0, chunk), :] + bb_ref[pl.ds(CTX_LEN + r0, chunk), :]
        o_ref[0, pl.ds(r0, chunk), :] = (hsum * jax.nn.gelu(yl_ref[0, pl.ds(r0, chunk), :].astype(F32))).astype(BF16)
        return 0

    lax.fori_loop(0, SEQ // chunk, out_body, 0)
    if need_ctx:
        hsum = bf_ref[0:CTX_LEN, :] + bb_ref[0:CTX_LEN, :]
        oc_ref[0] = (hsum * jax.nn.gelu(yc_ref[0].astype(F32))).astype(BF16)


def _rglru(proj_l, proj_c, conv_w, conv_b, wa, ba, wx, bx, lam, batch, need_ctx):
    def lat(sec):
        return pl.BlockSpec((1, SEQ, LANES), lambda b, k: (sec * NCB + k, b, 0))

    def ctx(sec):
        return pl.BlockSpec((1, CTX_LEN, LANES), lambda b, k: (sec * NCB + k, b, 0))

    def cols(rows):
        return pl.BlockSpec((rows, LANES), lambda b, k: (0, k))

    gate_w = pl.BlockSpec((2, 1, LANES, LANES), lambda b, k: (0, k, 0, 0))
    n_all = CTX_LEN + SEQ
    in_specs = [ctx(S_LX), lat(S_LX), lat(S_LY), cols(4), cols(1), gate_w, cols(2), gate_w, cols(2), cols(2)]
    args = [proj_c, proj_l, proj_l, conv_w, conv_b.reshape(1, D_MODEL), wa, ba, wx, bx, lam]
    out_specs = [pl.BlockSpec((1, SEQ, LANES), lambda b, k: (k, b, 0))]
    out_shape = [jax.ShapeDtypeStruct((LRU_BLOCKS, batch * SEQ, LANES), BF16)]
    if need_ctx:
        in_specs.append(ctx(S_LY))
        args.append(proj_c)
        out_specs.append(pl.BlockSpec((1, CTX_LEN, LANES), lambda b, k: (k, b, 0)))
        out_shape.append(jax.ShapeDtypeStruct((LRU_BLOCKS, batch * CTX_LEN, LANES), BF16))
    return pl.pallas_call(
        functools.partial(_lru_body, need_ctx=need_ctx),
        grid=(batch, LRU_BLOCKS),
        in_specs=in_specs,
        out_specs=out_specs,
        out_shape=out_shape,
        scratch_shapes=[pltpu.VMEM((CTX_LEN + 2 * SUBLANES, LANES), F32),
                        pltpu.VMEM((SEQ + 2 * SUBLANES, LANES), F32),
                        pltpu.VMEM((n_all, LANES), F32), pltpu.VMEM((n_all, LANES), F32),
                        pltpu.VMEM((n_all, LANES), F32), pltpu.VMEM((n_all, LANES), F32)],
        compiler_params=_params(("arbitrary", "arbitrary")),
        name="rglru",
    )(*args)


def _cat(ref):
    return jnp.concatenate([ref[g] for g in range(NCB)], axis=1)


def _merge_body(x_ref, mod_ref, g2_ref, a_ref, b_ref, c_ref, ga_ref, gb_ref, gc_ref, wb_ref, wo_ref,
                xo_ref, h2_ref):
    oa = jnp.dot(_cat(a_ref), wb_ref[0], preferred_element_type=F32)
    ob = jnp.dot(_cat(b_ref), wb_ref[1], preferred_element_type=F32)
    oc = jnp.dot(_cat(c_ref), wb_ref[2], preferred_element_type=F32)
    mixed = (jax.nn.sigmoid(_cat(ga_ref).astype(F32)) * oa + jax.nn.sigmoid(_cat(gb_ref).astype(F32)) * ob
             + jax.nn.sigmoid(_cat(gc_ref).astype(F32)) * oc)
    y = jnp.dot(mixed.astype(BF16), wo_ref[...], preferred_element_type=F32)
    xn = x_ref[...] + mod_ref[0, 2:3, :] * y
    xo_ref[...] = xn
    ms = jnp.mean(xn * xn, axis=-1, keepdims=True)
    hn = xn * lax.rsqrt(ms + NORM_EPS) * g2_ref[...]
    h2_ref[...] = hn * (1.0 + mod_ref[0, 4:5, :]) + mod_ref[0, 3:4, :]


def _merge(x, mod, mod_row, norm2_g, ret, na, lru, proj, wb, wo, tm):
    t = x.shape[0]

    def cb(sec):
        return pl.BlockSpec((NCB, tm, LANES), lambda i: (sec, i, 0))

    tok = pl.BlockSpec((tm, D_MODEL), lambda i: (i, 0))
    shape = jax.ShapeDtypeStruct((t, D_MODEL), F32)
    return pl.pallas_call(
        _merge_body,
        grid=(t // tm,),
        in_specs=[tok,
                  pl.BlockSpec((1, 6, D_MODEL), lambda i: (mod_row(i), 0, 0)),
                  pl.BlockSpec((1, D_MODEL), lambda i: (0, 0)),
                  cb(0), cb(0), cb(0), cb(S_GA), cb(S_GB), cb(S_GC),
                  pl.BlockSpec((3, D_MODEL, D_MODEL), lambda i: (0, 0, 0)),
                  pl.BlockSpec((D_MODEL, D_MODEL), lambda i: (0, 0))],
        out_specs=[tok, tok],
        out_shape=[shape, shape],
        compiler_params=_params(("arbitrary",)),
        name="merge",
    )(x, mod, norm2_g.reshape(1, D_MODEL), ret, na, lru, proj, proj, proj, wb, wo)


def _router_body(h_ref, wr_ref, br_ref, init_ref, ids_ref, w_ref, within_ref, cnt_ref, run_ref):
    tm = h_ref.shape[0]

    @pl.when(pl.program_id(0) == 0)
    def _():
        run_ref[...] = init_ref[...]

    logits = lax.dot_general(wr_ref[...], h_ref[...], (((1,), (1,)), ((), ())), preferred_element_type=F32,
                             precision=lax.Precision.HIGHEST) + br_ref[...]
    eidx = lax.broadcasted_iota(jnp.int32, (N_EXPERTS, tm), 0).astype(F32)
    cur = logits
    sel = jnp.zeros((N_EXPERTS, tm), F32)
    vals, hots = [], []
    for _ in range(TOP_K):
        m = jnp.max(cur, axis=0, keepdims=True)
        idx = jnp.min(jnp.where(cur == m, eidx, float(N_EXPERTS)), axis=0, keepdims=True)
        hot = eidx == idx
        vals.append(m)
        hots.append(hot)
        sel = sel + hot.astype(F32)
        cur = jnp.where(hot, -jnp.inf, cur)
    exps = [jnp.exp(v - vals[0]) for v in vals]
    den = functools.reduce(jnp.add, exps)

    ti = lax.broadcasted_iota(jnp.int32, (tm, tm), 0)
    tj = lax.broadcasted_iota(jnp.int32, (tm, tm), 1)
    upper = (ti < tj).astype(BF16)
    before = jnp.dot(sel.astype(BF16), upper, preferred_element_type=F32) + run_ref[:, 0:1]
    for k in range(TOP_K):
        ids_ref[k:k + 1, :] = jnp.sum(jnp.where(hots[k], eidx, 0.0), axis=0, keepdims=True).astype(jnp.int32)
        w_ref[k:k + 1, :] = exps[k] / den
        within_ref[k:k + 1, :] = jnp.sum(jnp.where(hots[k], before, 0.0), axis=0, keepdims=True).astype(jnp.int32)
    run_ref[...] = run_ref[...] + jnp.sum(sel, axis=1, keepdims=True)
    cnt_ref[...] = run_ref[...]


def _router(h2, w_router_t, b_router, init_counts, tm):
    t = h2.shape[0]
    tok4 = pl.BlockSpec((TOP_K, tm), lambda i: (0, i))
    cnt = pl.BlockSpec((N_EXPERTS, LANES), lambda i: (0, 0))
    return pl.pallas_call(
        _router_body,
        grid=(t // tm,),
        in_specs=[pl.BlockSpec((tm, D_MODEL), lambda i: (i, 0)),
                  pl.BlockSpec((N_EXPERTS, D_MODEL), lambda i: (0, 0)),
                  pl.BlockSpec((N_EXPERTS, 1), lambda i: (0, 0)),
                  cnt],
        out_specs=[tok4, tok4, tok4, cnt],
        out_shape=[jax.ShapeDtypeStruct((TOP_K, t), jnp.int32), jax.ShapeDtypeStruct((TOP_K, t), F32),
                   jax.ShapeDtypeStruct((TOP_K, t), jnp.int32),
                   jax.ShapeDtypeStruct((N_EXPERTS, LANES), F32)],
        scratch_shapes=[pltpu.VMEM((N_EXPERTS, LANES), F32)],
        compiler_params=_params(("arbitrary",)),
        name="router_topk",
    )(h2, w_router_t, b_router.reshape(N_EXPERTS, 1), init_counts)


def _row_copy(src_ref, src_row, dst_ref, dst_row, sem):
    return pltpu.make_async_copy(src_ref.at[pl.ds(pl.multiple_of(src_row * SUBLANES, SUBLANES), SUBLANES)],
                                 dst_ref.at[pl.ds(pl.multiple_of(dst_row * SUBLANES, SUBLANES), SUBLANES)], sem)


def _dispatch_body(pos_ref, h_ref, xg_in_ref, xg_ref, slab_ref, sem):
    del xg_in_ref
    tm = h_ref.shape[0]
    for s in range(NCB):
        slab_ref[pl.ds(s, tm, stride=SUBLANES), :] = h_ref[:, s * LANES:(s + 1) * LANES]

    def each_row_copy(fn):
        def body(i, _):
            for j in range(ROW_DMA_UNROLL):
                t = i * ROW_DMA_UNROLL + j
                for k in range(TOP_K):
                    fn(_row_copy(slab_ref, t, xg_ref, pos_ref[t * TOP_K + k], sem), k)
            return 0

        lax.fori_loop(0, tm // ROW_DMA_UNROLL, body, 0)

    each_row_copy(lambda cp, k: cp.start(priority=k % 2))
    each_row_copy(lambda cp, k: cp.wait())


def _dispatch(h2, pos, xg, tm):
    t = h2.shape[0]
    return pl.pallas_call(
        _dispatch_body,
        grid=(t // tm,),
        in_specs=[pl.BlockSpec((TOP_K * tm,), lambda i: (i,), memory_space=pltpu.SMEM),
                  pl.BlockSpec((tm, D_MODEL), lambda i: (i, 0)),
                  pl.BlockSpec(memory_space=pl.ANY)],
        out_specs=pl.BlockSpec(memory_space=pl.ANY),
        out_shape=jax.ShapeDtypeStruct(xg.shape, F32),
        scratch_shapes=[pltpu.VMEM((tm * SUBLANES, LANES), F32), pltpu.SemaphoreType.DMA],
        input_output_aliases={2: 0},
        compiler_params=_params(("arbitrary",)),
        name="moe_dispatch",
    )(pos, h2, xg)


DEINT_COLS = 2 * LANES
DEINT_ROWS = 512


def _deinterleave_matrix():
    src = np.arange(DEINT_COLS)[:, None]
    dst = np.arange(DEINT_COLS)[None, :]
    pick = np.where(dst < LANES, src == 2 * dst, src == 2 * (dst - LANES) + 1)
    return jnp.asarray(pick, BF16)


def _deint_body(w_ref, p_ref, g_ref, l_ref):
    for cidx in range(2 * D_MODEL // DEINT_COLS):
        chunk = w_ref[0, 0, :, cidx * DEINT_COLS:(cidx + 1) * DEINT_COLS].astype(BF16)
        r = jnp.dot(chunk, p_ref[...], preferred_element_type=F32)
        g_ref[0, :, cidx * LANES:(cidx + 1) * LANES] = r[:, :LANES].astype(BF16)
        l_ref[0, :, cidx * LANES:(cidx + 1) * LANES] = r[:, LANES:].astype(BF16)


def _split_up_projection(w_up, layer):
    out = pl.BlockSpec((1, DEINT_ROWS, D_MODEL), lambda e, i: (e, i, 0))
    shape = jax.ShapeDtypeStruct((N_EXPERTS, D_MODEL, D_MODEL), BF16)
    return pl.pallas_call(
        _deint_body,
        grid=(N_EXPERTS, D_MODEL // DEINT_ROWS),
        in_specs=[pl.BlockSpec((1, 1, DEINT_ROWS, 2 * D_MODEL), lambda e, i: (layer, e, i, 0)),
                  pl.BlockSpec((DEINT_COLS, DEINT_COLS), lambda e, i: (0, 0))],
        out_specs=[out, out],
        out_shape=[shape, shape],
        compiler_params=_params(("arbitrary", "arbitrary")),
        name="split_up_projection",
    )(w_up, _deinterleave_matrix())


def _expert_body(te_ref, nu_ref, x_ref, wg_ref, bg_ref, wl_ref, bl_ref, wd_ref, bd_ref, y_ref):
    del te_ref
    tm = x_ref.shape[0] // SUBLANES

    @pl.when(pl.program_id(0) < nu_ref[0])
    def _():
        x = jnp.concatenate([x_ref[pl.ds(s, tm, stride=SUBLANES), :] for s in range(NCB)], axis=1).astype(BF16)
        glu = jnp.minimum(jnp.dot(x, wg_ref[0], preferred_element_type=F32) + bg_ref[0], SWIGLU_LIMIT)
        lin = jnp.clip(jnp.dot(x, wl_ref[0], preferred_element_type=F32) + bl_ref[0], -SWIGLU_LIMIT, SWIGLU_LIMIT)
        act = glu * jax.nn.sigmoid(SWIGLU_ALPHA * glu) * (lin + 1.0)
        y = jnp.dot(act.astype(BF16), wd_ref[0], preferred_element_type=F32) + bd_ref[0]
        for s in range(NCB):
            y_ref[pl.ds(s, tm, stride=SUBLANES), :] = y[:, s * LANES:(s + 1) * LANES]

    @pl.when(pl.program_id(0) >= nu_ref[0])
    def _():
        y_ref[...] = jnp.zeros_like(y_ref)


def _experts(xg, tile_expert, n_used, wg, bg, wl, bl, wd, bd):
    tm = EXPERT_TILE
    n_tiles = xg.shape[0] // (tm * SUBLANES)

    def rows(m, te, nu):
        return (jnp.maximum(jnp.minimum(m, nu[0] - 1), 0), 0)

    def per_expert(shape):
        return pl.BlockSpec((1,) + shape, lambda m, te, nu: (te[m], 0, 0))

    grid_spec = pltpu.PrefetchScalarGridSpec(
        num_scalar_prefetch=2,
        grid=(n_tiles,),
        in_specs=[pl.BlockSpec((tm * SUBLANES, LANES), rows),
                  per_expert((D_MODEL, D_MODEL)), per_expert((1, D_MODEL)),
                  per_expert((D_MODEL, D_MODEL)), per_expert((1, D_MODEL)),
                  per_expert((D_MODEL, D_MODEL)), per_expert((1, D_MODEL))],
        out_specs=pl.BlockSpec((tm * SUBLANES, LANES), lambda m, te, nu: (m, 0)),
    )
    return pl.pallas_call(
        _expert_body,
        grid_spec=grid_spec,
        out_shape=jax.ShapeDtypeStruct(xg.shape, F32),
        compiler_params=_params(("arbitrary",)),
        name="moe_experts",
    )(tile_expert, n_used, xg, wg, bg, wl, bl, wd, bd)


def _combine_body(pos_ref, pos_next_ref, x_ref, w_ref, mod_ref, fg_ref, yg_ref, o_ref, ybuf_ref, sems, *,
                  final_norm):
    tm = x_ref.shape[0]
    i = pl.program_id(0)
    n = pl.num_programs(0)

    def for_each_row_copy(p_ref, slot, fn):
        def body(i, _):
            for j in range(ROW_DMA_UNROLL):
                t = i * ROW_DMA_UNROLL + j
                for k in range(TOP_K):
                    fn(_row_copy(yg_ref, p_ref[t * TOP_K + k], ybuf_ref.at[slot, k], t, sems.at[slot]), k)
            return 0

        lax.fori_loop(0, tm // ROW_DMA_UNROLL, body, 0)

    def step(slot):
        @pl.when(i == 0)
        def _():
            for_each_row_copy(pos_ref, slot, lambda cp, k: cp.start(priority=k % 2))

        @pl.when(i + 1 < n)
        def _():
            for_each_row_copy(pos_next_ref, 1 - slot, lambda cp, k: cp.start(priority=k % 2))

        for_each_row_copy(pos_ref, slot, lambda cp, k: cp.wait())
        acc = jnp.zeros((tm, D_MODEL), F32)
        for k in range(TOP_K):
            yk = jnp.concatenate([ybuf_ref[slot, k, pl.ds(s, tm, stride=SUBLANES), :] for s in range(NCB)], axis=1)
            acc = acc + w_ref[:, k:k + 1] * yk
        xn = x_ref[...] + mod_ref[0, 5:6, :] * acc
        if final_norm:
            ms = jnp.mean(xn * xn, axis=-1, keepdims=True)
            xn = xn * lax.rsqrt(ms + NORM_EPS) * fg_ref[...]
        o_ref[...] = xn

    for slot in range(2):
        pl.when(i % 2 == slot)(functools.partial(step, slot))


def _combine(x, w_tok, pos, yg, mod, mod_row, final_g, tm, final_norm):
    t = x.shape[0]
    n = t // tm
    return pl.pallas_call(
        functools.partial(_combine_body, final_norm=final_norm),
        grid=(n,),
        in_specs=[pl.BlockSpec((TOP_K * tm,), lambda i: (i,), memory_space=pltpu.SMEM),
                  pl.BlockSpec((TOP_K * tm,), lambda i: (jnp.minimum(i + 1, n - 1),), memory_space=pltpu.SMEM),
                  pl.BlockSpec((tm, D_MODEL), lambda i: (i, 0)),
                  pl.BlockSpec((tm, TOP_K), lambda i: (i, 0)),
                  pl.BlockSpec((1, 6, D_MODEL), lambda i: (mod_row(i), 0, 0)),
                  pl.BlockSpec((1, D_MODEL), lambda i: (0, 0)),
                  pl.BlockSpec(memory_space=pl.ANY)],
        out_specs=pl.BlockSpec((tm, D_MODEL), lambda i: (i, 0)),
        out_shape=jax.ShapeDtypeStruct((t, D_MODEL), F32),
        scratch_shapes=[pltpu.VMEM((2, TOP_K, tm * SUBLANES, LANES), F32), pltpu.SemaphoreType.DMA((2,))],
        compiler_params=_params(("arbitrary",)),
        name="moe_combine",
    )(pos, pos, x, w_tok, mod, final_g.reshape(1, D_MODEL), yg)


def _moe(streams, mod, w_router, b_router, w_up_all, layer, b_up, w_down, b_down, final_g, final_norm):
    te = EXPERT_TILE
    w_router_t = w_router.T
    counts = jnp.zeros((N_EXPERTS, LANES), F32)
    routed = []
    for _, h2, _ in streams:
        ids, w, within, counts = _router(h2, w_router_t, b_router, counts, min(512, h2.shape[0]))
        routed.append((ids, w, within))
    total = sum(h2.shape[0] for _, h2, _ in streams)
    cnt = counts[:, 0].astype(jnp.int32)
    padded = ((cnt + te - 1) // te) * te
    ends = jnp.cumsum(padded)
    starts = ends - padded
    n_tiles = total * TOP_K // te + N_EXPERTS
    tile_ids = jnp.arange(n_tiles, dtype=jnp.int32)
    tile_expert = jnp.minimum(jnp.sum((ends // te)[None, :] <= tile_ids[:, None], axis=1),
                              N_EXPERTS - 1).astype(jnp.int32)
    n_used = (ends[-1:] // te).astype(jnp.int32)
    expert_ids = jnp.arange(N_EXPERTS, dtype=jnp.int32)[:, None, None]

    wg, wl = _split_up_projection(w_up_all, layer)
    bg = b_up[:, None, 0::2]
    bl = b_up[:, None, 1::2]

    xg = jnp.zeros((n_tiles * te * SUBLANES, LANES), F32)
    positions = []
    for (_, h2, _), (ids, _, within) in zip(streams, routed):
        pos = within + jnp.sum(jnp.where(ids[None] == expert_ids, starts[:, None, None], 0), axis=0)
        pos = pos.T.reshape(-1)
        positions.append(pos)
        xg = _dispatch(h2, pos, xg, ROW_DMA_TILE)
    yg = _experts(xg, tile_expert, n_used, wg, bg, wl, bl, w_down.astype(BF16), b_down[:, None, :])
    return [_combine(x, w.T, pos, yg, mod, mod_row, final_g, ROW_DMA_TILE, final_norm)
            for (x, _, mod_row), (_, w, _), pos in zip(streams, routed, positions)]


def _rope_tables():
    tok = jnp.arange(SEQ)
    nf = 32
    inv = ROPE_BASE ** (-jnp.arange(nf, dtype=F32) / nf)
    ang_r = (tok // GRID_W).astype(F32)[:, None] * inv
    ang_c = (tok % GRID_W).astype(F32)[:, None] * inv
    cos = jnp.concatenate([jnp.cos(ang_r)] * 2 + [jnp.cos(ang_c)] * 2, axis=-1)
    sin = jnp.concatenate([-jnp.sin(ang_r), jnp.sin(ang_r), -jnp.sin(ang_c), jnp.sin(ang_c)], axis=-1)
    return cos, sin


def kernel(x, c, ctx, c_ctx, w_mod, b_mod, norm1_g, norm2_g, w_mix_in, ret_decay_fwd, ret_decay_bwd, na_rel_bias, lru_conv_w, lru_conv_b, lru_gate_a_w, lru_gate_a_b, lru_gate_x_w, lru_gate_x_b, lru_lambda, w_branch, w_mix_out, w_router, b_router, w_expert_up, b_expert_up, w_expert_down, b_expert_down, final_norm_g):
    batch = x.shape[0]
    depth = w_mod.shape[0]
    assert x.shape[1:] == (SEQ, D_MODEL) and ctx.shape[1:] == (CTX_LEN, D_MODEL)
    tm_l = TOKEN_TILE
    tm_c = min(TOKEN_TILE, batch * CTX_LEN)
    tm_ml = MERGE_TILE
    tm_mc = min(MERGE_TILE, batch * CTX_LEN)
    assert SEQ % tm_l == 0 and (batch * CTX_LEN) % tm_c == 0 and (batch * CTX_LEN) % tm_mc == 0

    def lat_row(tile_rows):
        return lambda i: i // (SEQ // tile_rows)

    def ctx_row(tile_rows):
        del tile_rows
        return lambda i: batch

    xl = x.reshape(batch * SEQ, D_MODEL)
    xc = ctx.reshape(batch * CTX_LEN, D_MODEL)
    mp = -(-(batch + 1) // SUBLANES) * SUBLANES
    cc = jnp.zeros((mp, D_MODEL), F32).at[:batch].set(c).at[batch].set(c_ctx)
    cos, sin_signed = _rope_tables()

    for l in range(depth):
        last = l == depth - 1
        need_ctx = not last
        mod = _modulation(cc, w_mod[l], b_mod[l])
        w_in = w_mix_in[l].astype(BF16)
        proj_l = _input_projection(xl, mod, lat_row(tm_l), norm1_g[l], w_in, tm_l)
        proj_c = _input_projection(xc, mod, ctx_row(tm_c), norm1_g[l], w_in, tm_c)
        lg = jnp.stack([jax.nn.log_sigmoid(ret_decay_fwd[l].astype(F32)),
                        jax.nn.log_sigmoid(ret_decay_bwd[l].astype(F32))])
        ret = _retention(proj_l, proj_c, lg, cos, sin_signed, batch, need_ctx)
        na = _neighbourhood(proj_l, proj_c, _na_bias_table(na_rel_bias[l]), batch, need_ctx)
        lru = _rglru(proj_l, proj_c, lru_conv_w[l], lru_conv_b[l], lru_gate_a_w[l].astype(BF16), lru_gate_a_b[l],
                     lru_gate_x_w[l].astype(BF16), lru_gate_x_b[l], lru_lambda[l], batch, need_ctx)
        wb = w_branch[l].astype(BF16)
        wo = w_mix_out[l].astype(BF16)
        xl, h2l = _merge(xl, mod, lat_row(tm_ml), norm2_g[l], ret[0], na[0], lru[0], proj_l, wb, wo, tm_ml)
        streams = [(xl, h2l, lat_row(ROW_DMA_TILE))]
        if need_ctx:
            xc, h2c = _merge(xc, mod, ctx_row(tm_mc), norm2_g[l], ret[1], na[1], lru[1], proj_c, wb, wo, tm_mc)
            streams = [(xc, h2c, ctx_row(ROW_DMA_TILE))] + streams
        outs = _moe(streams, mod, w_router[l], b_router[l], w_expert_up, l, b_expert_up[l], w_expert_down[l],
                    b_expert_down[l], final_norm_g, last)
        xl = outs[-1]
        if need_ctx:
            xc = outs[0]
    return xl.reshape(batch, SEQ, D_MODEL)
```
